```python
import jax, jax.numpy as jnp
from jax import lax
import numpy as np

D_MODEL = 1024
BATCH = 8
SEQ = 2048
DEPTH = 4
DEC_BATCH = 128
DEC_SEQ = 8
PAST_LEN = 2048
PAGE_SIZE = 128

N_MIXERS = 3
N_HEADS = 16
HEAD_DIM = D_MODEL // N_HEADS
ATTN_SCALE = HEAD_DIM ** -0.5
Q_BLOCK = 128
D_RNN = (4 * D_MODEL // 3) // 64 * 64
RG_BLOCKS = N_HEADS
RG_BS = D_RNN // RG_BLOCKS
CONV_W = 4
RG_C = 8.0
NSA_KV_GROUPS = 4
NSA_HPG = N_HEADS // NSA_KV_GROUPS
KV_W = NSA_KV_GROUPS * HEAD_DIM
CMP_BLOCK = 32
CMP_STRIDE = 16
CMP_HIDDEN = 4 * HEAD_DIM
SEL_BLOCK = 64
N_SELECT = 16
N_LOCAL_BLOCKS = 2
FORCE_BONUS = 1.0e4
WINDOW = 512
SEL_Q_BLOCK = 32
D_FF = 2816
N_EXPERTS = 8
TOP_K = 2
D_FF_EXPERT = 3584
DN_ALPHA = (2 * DEPTH) ** 0.25
DN_BETA = (8 * DEPTH) ** -0.25
LN_EPS = 1e-5
NEG = -1.0e30
N_FOX = (DEPTH + 2) // 3
N_RG = (DEPTH + 1) // 3
N_NSA = DEPTH // 3
N_DENSE = (DEPTH + 1) // 2
N_MOE = DEPTH // 2

kernel_name = 'hybrid_fox_rglru_nsa_deepnorm_step'


def layer_norm(x, g, b):
    xf = x.astype(jnp.float32)
    mu = jnp.mean(xf, -1, keepdims=True)
    var = jnp.mean(jnp.square(xf - mu), -1, keepdims=True)
    return ((xf - mu) * lax.rsqrt(var + LN_EPS) * g + b).astype(x.dtype)


def masked_softmax(logits, mask):
    logits = jnp.where(mask, logits.astype(jnp.float32), NEG)
    m = jnp.max(logits, -1, keepdims=True)
    p = jnp.exp(logits - m) * mask
    return p / jnp.maximum(jnp.sum(p, -1, keepdims=True), 1e-30)


def gather_pages(pool, slot, page_table):
    g = pool[slot, page_table]
    return g.reshape((g.shape[0], g.shape[1] * g.shape[2]) + g.shape[3:])


def fox_project(x, w_in, b_f):
    z = x @ w_in
    d = N_HEADS * HEAD_DIM
    q, k, v, f = jnp.split(z, [d, 2 * d, 3 * d], axis=-1)
    shp = x.shape[:2] + (N_HEADS, HEAD_DIM)
    logf = jax.nn.log_sigmoid((f + b_f).astype(jnp.float32))
    return q.reshape(shp), k.reshape(shp), v.reshape(shp), logf


def fox_attend(q, c_q, t_q, k, v, c_k, t_k):
    s = jnp.einsum('bqhd,bkhd->bhqk', q, k).astype(jnp.float32) * ATTN_SCALE
    s = s + jnp.transpose(c_q, (0, 2, 1))[..., :, None] - jnp.transpose(c_k, (0, 2, 1))[..., None, :]
    p = masked_softmax(s, t_k[None, :] <= t_q[:, None])
    return jnp.einsum('bhqk,bkhd->bqhd', p.astype(v.dtype), v)


def fox_prompt(x, w_in, b_f, w_out):
    B, T, _ = x.shape
    q, k, v, logf = fox_project(x, w_in, b_f)
    c = jnp.cumsum(logf, axis=1)
    t = jnp.arange(T)

    def blk(i):
        q0 = i * Q_BLOCK
        qb = lax.dynamic_slice_in_dim(q, q0, Q_BLOCK, 1)
        cb = lax.dynamic_slice_in_dim(c, q0, Q_BLOCK, 1)
        return fox_attend(qb, cb, q0 + jnp.arange(Q_BLOCK), k, v, c, t)

    o = lax.map(blk, jnp.arange(T // Q_BLOCK))
    o = jnp.moveaxis(o, 0, 1).reshape(B, T, -1)
    return o @ w_out, jnp.stack([k, v], axis=2), logf


def fox_sample(x, kv_pool, logf_pool, slot, page_table, w_in, b_f, w_out):
    B, T, _ = x.shape
    q, k, v, logf = fox_project(x, w_in, b_f)
    kv_past = gather_pages(kv_pool, slot, page_table)
    lf_past = gather_pages(logf_pool, slot, page_table).astype(jnp.float32)
    P = kv_past.shape[1]
    k_all = jnp.concatenate([kv_past[:, :, 0].astype(k.dtype), k], 1)
    v_all = jnp.concatenate([kv_past[:, :, 1].astype(v.dtype), v], 1)
    c_all = jnp.cumsum(jnp.concatenate([lf_past, logf], 1), axis=1)
    t_all = jnp.arange(P + T)
    o = fox_attend(q, c_all[:, P:], t_all[P:], k_all, v_all, c_all, t_all)
    return o.reshape(B, T, -1) @ w_out, jnp.stack([k, v], axis=2), logf


def rg_mixer(x, conv_buf, h0, w_in, conv_w, conv_b, ga_w, ga_b, gx_w, gx_b, lam, w_out):
    T = x.shape[1]
    gate_branch, u = jnp.split(x @ w_in, 2, axis=-1)
    up = jnp.concatenate([conv_buf.astype(u.dtype), u], 1)
    cx = conv_b + sum(up[:, j:j + T] * conv_w[j] for j in range(CONV_W))
    cb = cx.reshape(cx.shape[:2] + (RG_BLOCKS, RG_BS))
    r = jax.nn.sigmoid(jnp.einsum('btni,nij->btnj', cb, ga_w).reshape(cx.shape) + ga_b)
    i_g = jax.nn.sigmoid(jnp.einsum('btni,nij->btnj', cb, gx_w).reshape(cx.shape) + gx_b)
    log_a = -RG_C * r.astype(jnp.float32) * jax.nn.softplus(-lam.astype(jnp.float32))
    a = jnp.exp(log_a)
    b = jnp.sqrt(-jnp.expm1(2.0 * log_a)) * (i_g * cx).astype(jnp.float32)

    def step(h, ab):
        h = ab[0] * h + ab[1]
        return h, h

    hT, hs = lax.scan(step, h0.astype(jnp.float32), (jnp.swapaxes(a, 0, 1), jnp.swapaxes(b, 0, 1)))
    hs = jnp.swapaxes(hs, 0, 1).astype(x.dtype)
    y = (jax.nn.gelu(gate_branch) * hs) @ w_out
    return y, hT.astype(x.dtype), up[:, up.shape[1] - (CONV_W - 1):]


def nsa_project(x, w_in, b_gate):
    B, T = x.shape[:2]
    q, kv, g = jnp.split(x @ w_in, [D_MODEL, D_MODEL + 6 * KV_W], axis=-1)
    q = q.reshape(B, T, NSA_KV_GROUPS, NSA_HPG, HEAD_DIM)
    kv = kv.reshape(B, T, 6, NSA_KV_GROUPS, HEAD_DIM)
    g = jax.nn.sigmoid(g + b_gate).reshape(B, T, 3, NSA_KV_GROUPS, NSA_HPG)
    return q, kv, g


def compress_rows(xr, pe, w1, b1, w2):
    B, L = xr.shape[:2]
    r = CMP_BLOCK // CMP_STRIDE
    n16 = L // CMP_STRIDE
    n_cmp = n16 - r + 1
    ch = xr[:, :n16 * CMP_STRIDE].reshape(B, n16, CMP_STRIDE, NSA_KV_GROUPS, HEAD_DIM)
    blocks = jnp.concatenate([ch[:, m:m + n_cmp] for m in range(r)], axis=2) + pe[:, None, :]
    flat = jnp.swapaxes(blocks, 2, 3).reshape(B, n_cmp, NSA_KV_GROUPS, CMP_BLOCK * HEAD_DIM)
    return jax.nn.gelu(flat @ w1 + b1) @ w2


def select_blocks(p_cmp, t_q, n_sel):
    n_cmp = p_cmp.shape[-1]
    c0 = jnp.arange(n_cmp) * CMP_STRIDE
    s0 = jnp.arange(n_sel) * SEL_BLOCK
    overlap = ((c0[:, None] < s0[None, :] + SEL_BLOCK) & (c0[:, None] + CMP_BLOCK > s0[None, :])).astype(jnp.float32)
    imp = jnp.einsum('bqghn,ns->bgqs', p_cmp, overlap)
    cur = (t_q // SEL_BLOCK)[:, None]
    blk = jnp.arange(n_sel)[None, :]
    forced = (blk == 0) | (cur - blk < N_LOCAL_BLOCKS)
    score = jnp.where(blk <= cur, imp + jnp.where(forced, FORCE_BONUS, 0.0), NEG)
    top, idx = lax.top_k(score, N_SELECT)
    return idx, top > 0.5 * NEG


def sel_attend(q, t_q, idx, valid, ks, vs, q_chunk):
    B, Tq = q.shape[:2]
    G = NSA_KV_GROUPS
    nsb = ks.shape[1] // SEL_BLOCK
    kb = jnp.moveaxis(ks.reshape(B, nsb, SEL_BLOCK, G, HEAD_DIM), 3, 1)
    vb = jnp.moveaxis(vs.reshape(B, nsb, SEL_BLOCK, G, HEAD_DIM), 3, 1)
    nc = Tq // q_chunk
    n = N_SELECT * SEL_BLOCK
    qs = jnp.moveaxis(q.reshape((B, nc, q_chunk) + q.shape[2:]), 1, 0)
    ts = t_q.reshape(nc, q_chunk)
    ids = jnp.moveaxis(idx.reshape(B, G, nc, q_chunk, N_SELECT), 2, 0)
    vls = jnp.moveaxis(valid.reshape(B, G, nc, q_chunk, N_SELECT), 2, 0)
    take = jax.vmap(jax.vmap(lambda blocks, ix: blocks[ix]))

    def chunk(args):
        qc, tc, ic, vc = args
        gk = take(kb, ic).reshape(B, G, q_chunk, n, HEAD_DIM)
        gv = take(vb, ic).reshape(B, G, q_chunk, n, HEAD_DIM)
        pos = (ic[..., None] * SEL_BLOCK + jnp.arange(SEL_BLOCK)).reshape(B, G, q_chunk, n)
        ok = (pos <= tc[None, None, :, None]) & jnp.repeat(vc, SEL_BLOCK, axis=-1)
        s = jnp.einsum('bqghd,bgqnd->bgqhn', qc, gk).astype(jnp.float32) * ATTN_SCALE
        p = masked_softmax(s, ok[:, :, :, None, :])
        return jnp.einsum('bgqhn,bgqnd->bqghd', p.astype(gv.dtype), gv)

    o = lax.map(chunk, (qs, ts, ids, vls))
    return jnp.moveaxis(o, 0, 1).reshape(q.shape)


def window_attend(q, t_q, k, v, t_k):
    s = jnp.einsum('bqghd,bkgd->bqghk', q, k).astype(jnp.float32) * ATTN_SCALE
    dt = t_q[:, None] - t_k[None, :]
    ok = (t_k[None, :] >= 0) & (dt >= 0) & (dt <= WINDOW)
    p = masked_softmax(s, ok[:, None, None, :])
    return jnp.einsum('bqghk,bkgd->bqghd', p.astype(v.dtype), v)


def window_prompt(q, kw, vw):
    T = q.shape[1]
    pad = ((0, 0), (WINDOW, 0), (0, 0), (0, 0))
    kp, vp = jnp.pad(kw, pad), jnp.pad(vw, pad)
    span = WINDOW + Q_BLOCK

    def blk(i):
        q0 = i * Q_BLOCK
        qb = lax.dynamic_slice_in_dim(q, q0, Q_BLOCK, 1)
        kb = lax.dynamic_slice_in_dim(kp, q0, span, 1)
        vb = lax.dynamic_slice_in_dim(vp, q0, span, 1)
        return window_attend(qb, q0 + jnp.arange(Q_BLOCK), kb, vb, q0 - WINDOW + jnp.arange(span))

    o = lax.map(blk, jnp.arange(T // Q_BLOCK))
    return jnp.moveaxis(o, 0, 1).reshape(q.shape)


def cmp_and_sel(q, t_q, kv4, q_chunk, cmp_pe, cmp_w1, cmp_b1, cmp_w2):
    L = kv4.shape[1]
    kc = compress_rows(kv4[:, :, 0], cmp_pe[0], cmp_w1[0], cmp_b1[0], cmp_w2[0])
    vc = compress_rows(kv4[:, :, 1], cmp_pe[1], cmp_w1[1], cmp_b1[1], cmp_w2[1])
    n_cmp = kc.shape[1]
    s = jnp.einsum('bqghd,bngd->bqghn', q, kc).astype(jnp.float32) * ATTN_SCALE
    end = jnp.arange(n_cmp) * CMP_STRIDE + CMP_BLOCK - 1
    p = masked_softmax(s, (end[None, :] <= t_q[:, None])[:, None, None, :])
    o_c = jnp.einsum('bqghn,bngd->bqghd', p.astype(vc.dtype), vc)
    n_sel = max(-(-L // SEL_BLOCK), N_SELECT)
    idx, valid = select_blocks(p, t_q, n_sel)
    pad = ((0, 0), (0, n_sel * SEL_BLOCK - L), (0, 0), (0, 0))
    o_s = sel_attend(q, t_q, idx, valid, jnp.pad(kv4[:, :, 2], pad), jnp.pad(kv4[:, :, 3], pad), q_chunk)
    return o_c, o_s


def nsa_merge(o_c, o_s, o_w, g):
    y = g[:, :, 0, ..., None] * o_c + g[:, :, 1, ..., None] * o_s + g[:, :, 2, ..., None] * o_w
    return y.reshape(y.shape[:2] + (-1,))


def nsa_prompt(x, w_in, b_gate, cmp_pe, cmp_w1, cmp_b1, cmp_w2, w_out):
    T = x.shape[1]
    q, kv, g = nsa_project(x, w_in, b_gate)
    o_c, o_s = cmp_and_sel(q, jnp.arange(T), kv[:, :, :4], SEL_Q_BLOCK, cmp_pe, cmp_w1, cmp_b1, cmp_w2)
    o_w = window_prompt(q, kv[:, :, 4], kv[:, :, 5])
    y = nsa_merge(o_c, o_s, o_w, g) @ w_out
    return y, kv[:, :, :4], kv[:, T - min(WINDOW, T):, 4:]


def nsa_sample(x, kv_pool, win_buf, slot, page_table, w_in, b_gate, cmp_pe, cmp_w1, cmp_b1, cmp_w2, w_out):
    T = x.shape[1]
    q, kv, g = nsa_project(x, w_in, b_gate)
    kv_past = gather_pages(kv_pool, slot, page_table).astype(kv.dtype)
    P = kv_past.shape[1]
    t_q = P + jnp.arange(T)
    kv4 = jnp.concatenate([kv_past, kv[:, :, :4]], 1)
    o_c, o_s = cmp_and_sel(q, t_q, kv4, 1, cmp_pe, cmp_w1, cmp_b1, cmp_w2)
    W = win_buf.shape[1]
    wk = jnp.concatenate([win_buf.astype(kv.dtype), kv[:, :, 4:]], 1)
    o_w = window_attend(q, t_q, wk[:, :, 0], wk[:, :, 1], P - W + jnp.arange(W + T))
    y = nsa_merge(o_c, o_s, o_w, g) @ w_out
    return y, kv[:, :, :4], wk[:, T:]


def swiglu(x, w_gu, w_down):
    gt, up = jnp.split(x @ w_gu, 2, axis=-1)
    return (jax.nn.silu(gt) * up) @ w_down


def moe_swiglu(x, router_w, router_b, w_gu, w_down):
    logits = (x @ router_w).astype(jnp.float32) + router_b
    top, idx = lax.top_k(logits, TOP_K)
    wts = jax.nn.softmax(top, axis=-1)
    gate = jnp.sum(jax.nn.one_hot(idx, N_EXPERTS, dtype=jnp.float32) * wts[..., None], axis=-2)
    y = jnp.zeros_like(x)
    for e in range(N_EXPERTS):
        y = y + gate[..., e:e + 1].astype(x.dtype) * swiglu(x, w_gu[e], w_down[e])
    return y


def setup_inputs(seed: int = 0) -> dict:
    key = jax.random.key(seed)
    keys = jax.random.split(key, 64)
    kit = iter(range(64))

    def nrm(shape, scale):
        return jax.random.normal(keys[next(kit)], shape, jnp.float32) * scale

    def uni(shape, lo, hi):
        return jax.random.uniform(keys[next(kit)], shape, jnp.float32, lo, hi)

    d = D_MODEL
    n_pages = PAST_LEN // PAGE_SIZE
    n_used = DEC_BATCH * n_pages
    n_pool = n_used + (n_used + 3) // 4
    w_buf = min(WINDOW, PAST_LEN)
    inp = {}
    inp['x_prompt'] = nrm((BATCH, SEQ, d), 1.0)
    inp['x_sample'] = nrm((DEC_BATCH, DEC_SEQ, d), 1.0)
    inp['cache_fox_kv'] = nrm((N_FOX, n_pool, PAGE_SIZE, 2, N_HEADS, HEAD_DIM), 1.0)
    inp['cache_fox_logf'] = jax.nn.log_sigmoid(uni((N_FOX, n_pool, PAGE_SIZE, N_HEADS), 1.0, 6.0)
                                               + nrm((N_FOX, n_pool, PAGE_SIZE, N_HEADS), 1.0))
    inp['state_rg_h'] = nrm((N_RG, DEC_BATCH, D_RNN), 0.5)
    inp['state_rg_conv'] = nrm((N_RG, DEC_BATCH, CONV_W - 1, D_RNN), 1.0)
    inp['cache_nsa_kv'] = nrm((N_NSA, n_pool, PAGE_SIZE, 4, NSA_KV_GROUPS, HEAD_DIM), 1.0)
    inp['state_nsa_win'] = nrm((N_NSA, DEC_BATCH, w_buf, 2, NSA_KV_GROUPS, HEAD_DIM), 1.0)
    perm = jax.random.permutation(keys[next(kit)], n_pool)
    inp['page_table'] = perm[:n_used].reshape(DEC_BATCH, n_pages).astype(jnp.int32)
    inp['fox_w_in'] = nrm((N_FOX, d, 3 * d + N_HEADS), d ** -0.5)
    inp['fox_b_f'] = uni((N_FOX, N_HEADS), 1.0, 6.0)
    inp['fox_w_out'] = nrm((N_FOX, d, d), d ** -0.5 * DN_BETA)
    inp['rg_w_in'] = nrm((N_RG, d, 2 * D_RNN), d ** -0.5)
    inp['rg_conv_w'] = nrm((N_RG, CONV_W, D_RNN), CONV_W ** -0.5)
    inp['rg_conv_b'] = nrm((N_RG, D_RNN), 0.01)
    inp['rg_gate_a_w'] = nrm((N_RG, RG_BLOCKS, RG_BS, RG_BS), RG_BS ** -0.5)
    inp['rg_gate_a_b'] = nrm((N_RG, D_RNN), 0.01)
    inp['rg_gate_x_w'] = nrm((N_RG, RG_BLOCKS, RG_BS, RG_BS), RG_BS ** -0.5)
    inp['rg_gate_x_b'] = nrm((N_RG, D_RNN), 0.01)
    a0 = uni((N_RG, D_RNN), 0.9, 0.999) ** (1.0 / RG_C)
    inp['rg_lambda'] = jnp.log(a0) - jnp.log1p(-a0)
    inp['rg_w_out'] = nrm((N_RG, D_RNN, d), D_RNN ** -0.5 * DN_BETA)
    inp['nsa_w_in'] = nrm((N_NSA, d, d + 6 * KV_W + 3 * N_HEADS), d ** -0.5)
    inp['nsa_b_gate'] = nrm((N_NSA, 3 * N_HEADS), 0.01)
    inp['nsa_cmp_pe'] = nrm((N_NSA, 2, CMP_BLOCK, HEAD_DIM), 0.1)
    inp['nsa_cmp_w1'] = nrm((N_NSA, 2, CMP_BLOCK * HEAD_DIM, CMP_HIDDEN), (CMP_BLOCK * HEAD_DIM) ** -0.5)
    inp['nsa_cmp_b1'] = nrm((N_NSA, 2, CMP_HIDDEN), 0.01)
    inp['nsa_cmp_w2'] = nrm((N_NSA, 2, CMP_HIDDEN, HEAD_DIM), CMP_HIDDEN ** -0.5)
    inp['nsa_w_out'] = nrm((N_NSA, d, d), d ** -0.5 * DN_BETA)
    inp['ffn_w_gu'] = nrm((N_DENSE, d, 2 * D_FF), d ** -0.5)
    inp['ffn_w_down'] = nrm((N_DENSE, D_FF, d), D_FF ** -0.5 * DN_BETA)
    inp['moe_router_w'] = nrm((N_MOE, d, N_EXPERTS), d ** -0.5)
    inp['moe_router_b'] = nrm((N_MOE, N_EXPERTS), 0.01)
    inp['moe_w_gu'] = nrm((N_MOE, N_EXPERTS, d, 2 * D_FF_EXPERT), d ** -0.5)
    inp['moe_w_down'] = nrm((N_MOE, N_EXPERTS, D_FF_EXPERT, d), D_FF_EXPERT ** -0.5 * DN_BETA)
    inp['ln_mix_g'] = 1.0 + nrm((DEPTH, d), 0.02)
    inp['ln_mix_b'] = nrm((DEPTH, d), 0.02)
    inp['ln_ffn_g'] = 1.0 + nrm((DEPTH, d), 0.02)
    inp['ln_ffn_b'] = nrm((DEPTH, d), 0.02)
    return inp


def reference(x_prompt, x_sample, cache_fox_kv, cache_fox_logf, state_rg_h, state_rg_conv,
              cache_nsa_kv, state_nsa_win, page_table,
              fox_w_in, fox_b_f, fox_w_out,
              rg_w_in, rg_conv_w, rg_conv_b, rg_gate_a_w, rg_gate_a_b, rg_gate_x_w, rg_gate_x_b,
              rg_lambda, rg_w_out,
              nsa_w_in, nsa_b_gate, nsa_cmp_pe, nsa_cmp_w1, nsa_cmp_b1, nsa_cmp_w2, nsa_w_out,
              ffn_w_gu, ffn_w_down, moe_router_w, moe_router_b, moe_w_gu, moe_w_down,
              ln_mix_g, ln_mix_b, ln_ffn_g, ln_ffn_b):
    xp, xs = x_prompt, x_sample
    bp = xp.shape[0]
    fkv_p, fkv_s, flf_p, flf_s = [], [], [], []
    rh_p, rh_s, rc_p, rc_s = [], [], [], []
    nkv_p, nkv_s, nw_p, nw_s = [], [], [], []
    for i in range(DEPTH):
        kind, slot = i % N_MIXERS, i // N_MIXERS
        if kind == 0:
            fw = (fox_w_in[slot], fox_b_f[slot], fox_w_out[slot])
            mp, kv_n, lf_n = fox_prompt(xp, *fw)
            fkv_p.append(kv_n)
            flf_p.append(lf_n)
            ms, kv_n, lf_n = fox_sample(xs, cache_fox_kv, cache_fox_logf, slot, page_table, *fw)
            fkv_s.append(kv_n)
            flf_s.append(lf_n)
        elif kind == 1:
            rw = (rg_w_in[slot], rg_conv_w[slot], rg_conv_b[slot], rg_gate_a_w[slot], rg_gate_a_b[slot],
                  rg_gate_x_w[slot], rg_gate_x_b[slot], rg_lambda[slot], rg_w_out[slot])
            mp, h_n, c_n = rg_mixer(xp, jnp.zeros((bp, CONV_W - 1, D_RNN), xp.dtype),
                                    jnp.zeros((bp, D_RNN), jnp.float32), *rw)
            rh_p.append(h_n)
            rc_p.append(c_n)
            ms, h_n, c_n = rg_mixer(xs, state_rg_conv[slot], state_rg_h[slot], *rw)
            rh_s.append(h_n)
            rc_s.append(c_n)
        else:
            nw = (nsa_w_in[slot], nsa_b_gate[slot], nsa_cmp_pe[slot], nsa_cmp_w1[slot],
                  nsa_cmp_b1[slot], nsa_cmp_w2[slot], nsa_w_out[slot])
            mp, kv_n, w_n = nsa_prompt(xp, *nw)
            nkv_p.append(kv_n)
            nw_p.append(w_n)
            ms, kv_n, w_n = nsa_sample(xs, cache_nsa_kv, state_nsa_win[slot], slot, page_table, *nw)
            nkv_s.append(kv_n)
            nw_s.append(w_n)
        xp = layer_norm(DN_ALPHA * xp + mp, ln_mix_g[i], ln_mix_b[i])
        xs = layer_norm(DN_ALPHA * xs + ms, ln_mix_g[i], ln_mix_b[i])
        j = i // 2
        if i % 2 == 0:
            fp = swiglu(xp, ffn_w_gu[j], ffn_w_down[j])
            fs = swiglu(xs, ffn_w_gu[j], ffn_w_down[j])
        else:
            fp = moe_swiglu(xp, moe_router_w[j], moe_router_b[j], moe_w_gu[j], moe_w_down[j])
            fs = moe_swiglu(xs, moe_router_w[j], moe_router_b[j], moe_w_gu[j], moe_w_down[j])
        xp = layer_norm(DN_ALPHA * xp + fp, ln_ffn_g[i], ln_ffn_b[i])
        xs = layer_norm(DN_ALPHA * xs + fs, ln_ffn_g[i], ln_ffn_b[i])
    return (xp, xs,
            jnp.stack(fkv_p), jnp.stack(fkv_s), jnp.stack(flf_p), jnp.stack(flf_s),
            jnp.stack(rh_p), jnp.stack(rh_s), jnp.stack(rc_p), jnp.stack(rc_s),
            jnp.stack(nkv_p), jnp.stack(nkv_s), jnp.stack(nw_p), jnp.stack(nw_s))
```

```python
import functools

import jax
import jax.numpy as jnp
from jax import lax
from jax.experimental import pallas as pl
from jax.experimental.pallas import tpu as pltpu

F32 = jnp.float32
BF16 = jnp.bfloat16

D_MODEL = 1024
DEPTH = 4
N_HEADS = 16
HEAD_DIM = 64
ATTN_SCALE = HEAD_DIM ** -0.5
Q_BLOCK = 128
D_RNN = 1344
RG_BLOCKS = 16
RG_BS = 84
RG_BSP = 88
D_RNNP = RG_BLOCKS * RG_BSP
CONV_W = 4
RG_C = 8.0
NSA_G = 4
NSA_HPG = 4
KV_W = NSA_G * HEAD_DIM
CMP_BLOCK = 32
CMP_STRIDE = 16
CMP_HIDDEN = 256
SEL_BLOCK = 64
N_SELECT = 16
N_LOCAL_BLOCKS = 2
FORCE_BONUS = 1.0e4
WINDOW = 512
D_FF = 2816
N_EXPERTS = 8
D_FF_EXPERT = 3584
DN_ALPHA = (2 * DEPTH) ** 0.25
LN_EPS = 1e-5
NEG = -1.0e30
LANE = 128
TOKEN_TILE = 1024
VMEM_LIMIT = 56 * 1024 * 1024


def _cparams(sem):
    return pltpu.CompilerParams(dimension_semantics=sem, vmem_limit_bytes=VMEM_LIMIT)


def _log_sigmoid(x):
    return jnp.minimum(x, 0.0) - jnp.log1p(jnp.exp(-jnp.abs(x)))


def _proj_kernel(x_ref, w_ref, b_ref, *out_refs, segs):
    xb = x_ref[...].astype(BF16)
    oi = 0
    for start, width, act, outs in segs:
        z = jnp.dot(xb, w_ref[:, start:start + width], preferred_element_type=F32)
        if act is not None:
            z = z + b_ref[:, start:start + width]
            z = _log_sigmoid(z) if act == "log_sigmoid" else jax.nn.sigmoid(z)
        for dtype, scale in outs:
            out_refs[oi][...] = (z * scale if scale != 1.0 else z).astype(dtype)
            oi += 1


def proj(x, w, bias, segs, tm=512):
    nt, d = x.shape
    n = w.shape[1]
    out_shape, out_specs = [], []
    for _, width, _, outs in segs:
        for dtype, _ in outs:
            out_shape.append(jax.ShapeDtypeStruct((nt, width), dtype))
            out_specs.append(pl.BlockSpec((tm, width), lambda i: (i, 0)))
    return pl.pallas_call(
        functools.partial(_proj_kernel, segs=segs),
        grid=(nt // tm,),
        in_specs=[pl.BlockSpec((tm, d), lambda i: (i, 0)),
                  pl.BlockSpec((d, n), lambda i: (0, 0)),
                  pl.BlockSpec((1, n), lambda i: (0, 0))],
        out_specs=out_specs,
        out_shape=out_shape,
        compiler_params=_cparams(("parallel",)),
        name="proj",
    )(x, w, bias)


def _layer_norm(y, g, b):
    mu = jnp.mean(y, axis=-1, keepdims=True)
    yc = y - mu
    var = jnp.mean(yc * yc, axis=-1, keepdims=True)
    return yc * lax.rsqrt(var + LN_EPS) * g + b


def _top2(logits, n_valid):
    lane = lax.broadcasted_iota(jnp.int32, logits.shape, 1)
    l = jnp.where(lane < n_valid, logits, -jnp.inf)
    m1 = jnp.max(l, axis=-1, keepdims=True)
    i1 = jnp.min(jnp.where(l == m1, lane, LANE), axis=-1, keepdims=True)
    l2 = jnp.where(lane == i1, -jnp.inf, l)
    m2 = jnp.max(l2, axis=-1, keepdims=True)
    i2 = jnp.min(jnp.where(l2 == m2, lane, LANE), axis=-1, keepdims=True)
    return m1, i1, m2, i2, lane


def _out_ln_kernel(*refs, n_act, router):
    acts = refs[:n_act]
    w_ref, x_ref, g_ref, b_ref = refs[n_act:n_act + 4]
    rest = refs[n_act + 4:]
    a = acts[0][...]
    for r in acts[1:]:
        a = (a.astype(F32) + r[...].astype(F32)).astype(BF16)
    m = jnp.dot(a, w_ref[...], preferred_element_type=F32)
    y = _layer_norm(DN_ALPHA * x_ref[...] + m, g_ref[...], b_ref[...])
    if not router:
        rest[0][...] = y
        return
    rw_ref, rb_ref, y_ref, idx_ref, wt_ref = rest
    y_ref[...] = y
    logits = jnp.dot(y, rw_ref[...], preferred_element_type=F32,
                     precision=lax.Precision.HIGHEST) + rb_ref[...]
    m1, i1, m2, i2, lane = _top2(logits, N_EXPERTS)
    e = jnp.exp(m2 - m1)
    den = 1.0 + e
    idx_ref[...] = jnp.where(lane == 0, i1, jnp.where(lane == 1, i2, 0))
    wt_ref[...] = jnp.where(lane == 0, 1.0 / den, jnp.where(lane == 1, e / den, 0.0))


def out_ln(acts, w, x, g, b, router=None, tm=512):
    nt, d = x.shape
    k = w.shape[0]
    row = lambda i: (i, 0)
    fix = lambda i: (0, 0)
    in_specs = [pl.BlockSpec((tm, k), row) for _ in acts]
    in_specs += [pl.BlockSpec((k, d), fix), pl.BlockSpec((tm, d), row),
                 pl.BlockSpec((1, d), fix), pl.BlockSpec((1, d), fix)]
    args = list(acts) + [w, x, g, b]
    out_shape = [jax.ShapeDtypeStruct((nt, d), F32)]
    out_specs = [pl.BlockSpec((tm, d), row)]
    if router is not None:
        in_specs += [pl.BlockSpec((d, LANE), fix), pl.BlockSpec((1, LANE), fix)]
        args += list(router)
        out_shape += [jax.ShapeDtypeStruct((nt, LANE), jnp.int32),
                      jax.ShapeDtypeStruct((nt, LANE), F32)]
        out_specs += [pl.BlockSpec((tm, LANE), row), pl.BlockSpec((tm, LANE), row)]
    res = pl.pallas_call(
        functools.partial(_out_ln_kernel, n_act=len(acts), router=router is not None),
        grid=(nt // tm,),
        in_specs=in_specs, out_specs=out_specs, out_shape=out_shape,
        compiler_params=_cparams(("parallel",)),
        name="out_ln",
    )(*args)
    return res if router is not None else res[0]


def _ffn_kernel(x_ref, wg_ref, wu_ref, wd_ref, g_ref, b_ref, o_ref, xb_ref, acc_ref):
    c = pl.program_id(1)

    @pl.when(c == 0)
    def _():
        xb_ref[...] = x_ref[...].astype(BF16)
        acc_ref[...] = jnp.zeros_like(acc_ref)

    xb = xb_ref[...]
    gt = jnp.dot(xb, wg_ref[...], preferred_element_type=F32)
    up = jnp.dot(xb, wu_ref[...], preferred_element_type=F32)
    h = (gt * jax.nn.sigmoid(gt) * up).astype(BF16)
    acc_ref[...] += jnp.dot(h, wd_ref[...], preferred_element_type=F32)

    @pl.when(c == pl.num_programs(1) - 1)
    def _():
        o_ref[...] = _layer_norm(DN_ALPHA * x_ref[...] + acc_ref[...], g_ref[...], b_ref[...])


def ffn_ln(x, w_gu, w_down, g, b, tm=1024, tf=256):
    nt, d = x.shape
    dff = w_down.shape[0]
    nc = dff // tf
    return pl.pallas_call(
        _ffn_kernel,
        grid=(nt // tm, nc),
        in_specs=[pl.BlockSpec((tm, d), lambda i, c: (i, 0)),
                  pl.BlockSpec((d, tf), lambda i, c: (0, c)),
                  pl.BlockSpec((d, tf), lambda i, c: (0, nc + c)),
                  pl.BlockSpec((tf, d), lambda i, c: (c, 0)),
                  pl.BlockSpec((1, d), lambda i, c: (0, 0)),
                  pl.BlockSpec((1, d), lambda i, c: (0, 0))],
        out_specs=pl.BlockSpec((tm, d), lambda i, c: (i, 0)),
        out_shape=jax.ShapeDtypeStruct((nt, d), F32),
        scratch_shapes=[pltpu.VMEM((tm, d), BF16), pltpu.VMEM((tm, d), F32)],
        compiler_params=_cparams(("parallel", "arbitrary")),
        name="ffn_ln",
    )(x, w_gu, w_gu, w_down, g, b)


MOE_ROWS = 512
MOE_TF = 512
DISPATCH_CHUNK = 2048


def _dispatch_kernel(pos_ref, x_hbm, xs_hbm, sem):
    base = pl.program_id(0) * DISPATCH_CHUNK

    def row_copy(r):
        return pltpu.make_async_copy(x_hbm.at[pl.ds(r // 2, 1)],
                                     xs_hbm.at[pl.ds(pos_ref[r], 1)], sem)

    def start(k, carry):
        row_copy(base + k).start()
        return carry

    def wait(k, carry):
        row_copy(base + k).wait()
        return carry

    lax.fori_loop(0, DISPATCH_CHUNK, start, 0)
    lax.fori_loop(0, DISPATCH_CHUNK, wait, 0)


def moe_dispatch(x, pos):
    nt, d = x.shape
    r = pos.shape[0]
    return pl.pallas_call(
        _dispatch_kernel,
        grid_spec=pltpu.PrefetchScalarGridSpec(
            num_scalar_prefetch=1, grid=(r // DISPATCH_CHUNK,),
            in_specs=[pl.BlockSpec(memory_space=pl.ANY)],
            out_specs=pl.BlockSpec(memory_space=pl.ANY),
            scratch_shapes=[pltpu.SemaphoreType.DMA(())]),
        out_shape=jax.ShapeDtypeStruct((r, d), F32),
        compiler_params=pltpu.CompilerParams(dimension_semantics=("arbitrary",),
                                             has_side_effects=True),
        name="moe_dispatch",
    )(pos, x)


def _moe_kernel(tile_ref, exp_ref, lo_ref, hi_ref, first_ref,
                xs_ref, wg_ref, wu_ref, wd_ref, y_ref, xb_ref, acc_ref):
    w = pl.program_id(0)
    c = pl.program_id(1)
    lo = lo_ref[w]
    hi = hi_ref[w]

    @pl.when(hi > lo)
    def _():
        @pl.when(c == 0)
        def _():
            xb_ref[...] = xs_ref[...].astype(BF16)
            acc_ref[...] = jnp.zeros_like(acc_ref)

        xb = xb_ref[...]
        gt = jnp.dot(xb, wg_ref[...], preferred_element_type=F32)
        up = jnp.dot(xb, wu_ref[...], preferred_element_type=F32)
        h = (gt * jax.nn.sigmoid(gt) * up).astype(BF16)
        acc_ref[...] += jnp.dot(h, wd_ref[...], preferred_element_type=F32)

        @pl.when(c == pl.num_programs(1) - 1)
        def _():
            rows = lax.broadcasted_iota(jnp.int32, acc_ref.shape, 0)
            mine = (rows >= lo) & (rows < hi)

            @pl.when(first_ref[w] == 1)
            def _():
                y_ref[...] = jnp.where(mine, acc_ref[...], 0.0)

            @pl.when(first_ref[w] == 0)
            def _():
                y_ref[...] = jnp.where(mine, acc_ref[...], y_ref[...])


def moe_experts(xs, meta, w_gu, w_down):
    r, d = xs.shape
    n_items = meta[0].shape[0]
    nc = D_FF_EXPERT // MOE_TF
    return pl.pallas_call(
        _moe_kernel,
        grid_spec=pltpu.PrefetchScalarGridSpec(
            num_scalar_prefetch=5, grid=(n_items, nc),
            in_specs=[
                pl.BlockSpec((MOE_ROWS, d), lambda w, c, t, e, lo, hi, f: (t[w], 0)),
                pl.BlockSpec((None, d, MOE_TF), lambda w, c, t, e, lo, hi, f: (e[w], 0, c)),
                pl.BlockSpec((None, d, MOE_TF), lambda w, c, t, e, lo, hi, f: (e[w], 0, nc + c)),
                pl.BlockSpec((None, MOE_TF, d), lambda w, c, t, e, lo, hi, f: (e[w], c, 0)),
            ],
            out_specs=pl.BlockSpec((MOE_ROWS, d), lambda w, c, t, e, lo, hi, f: (t[w], 0)),
            scratch_shapes=[pltpu.VMEM((MOE_ROWS, d), BF16), pltpu.VMEM((MOE_ROWS, d), F32)]),
        out_shape=jax.ShapeDtypeStruct((r, d), F32),
        compiler_params=_cparams(("arbitrary", "arbitrary")),
        name="moe_experts",
    )(*meta, xs, w_gu, w_gu, w_down)


def _combine_kernel(pos_ref, y_hbm, wt_ref, x_ref, g_ref, b_ref, o_ref, buf_ref, sem, *, tm):
    base = pl.program_id(0) * (2 * tm)

    def row_copy(k):
        return pltpu.make_async_copy(y_hbm.at[pl.ds(pos_ref[base + k], 1)],
                                     buf_ref.at[k % 2, pl.ds(k // 2, 1)], sem)

    def start(k, carry):
        row_copy(k).start()
        return carry

    def wait(k, carry):
        row_copy(k).wait()
        return carry

    lax.fori_loop(0, 2 * tm, start, 0)
    lax.fori_loop(0, 2 * tm, wait, 0)
    wt = wt_ref[...]
    y = wt[:, 0:1] * buf_ref[0] + wt[:, 1:2] * buf_ref[1]
    o_ref[...] = _layer_norm(DN_ALPHA * x_ref[...] + y, g_ref[...], b_ref[...])


def moe_combine_ln(y, pos, wt, x, g, b, tm=512):
    nt, d = x.shape
    return pl.pallas_call(
        functools.partial(_combine_kernel, tm=tm),
        grid_spec=pltpu.PrefetchScalarGridSpec(
            num_scalar_prefetch=1, grid=(nt // tm,),
            in_specs=[pl.BlockSpec(memory_space=pl.ANY),
                      pl.BlockSpec((tm, LANE), lambda i, p: (i, 0)),
                      pl.BlockSpec((tm, d), lambda i, p: (i, 0)),
                      pl.BlockSpec((1, d), lambda i, p: (0, 0)),
                      pl.BlockSpec((1, d), lambda i, p: (0, 0))],
            out_specs=pl.BlockSpec((tm, d), lambda i, p: (i, 0)),
            scratch_shapes=[pltpu.VMEM((2, tm, d), F32), pltpu.SemaphoreType.DMA(())]),
        out_shape=jax.ShapeDtypeStruct((nt, d), F32),
        compiler_params=_cparams(("arbitrary",)),
        name="moe_combine_ln",
    )(pos, y, wt, x, g, b)


def _moe_plan(idx):
    nt = idx.shape[0]
    r = 2 * nt
    e_flat = idx[:, :2].reshape(r)
    onehot = (e_flat[:, None] == jnp.arange(N_EXPERTS, dtype=jnp.int32)[None, :]).astype(jnp.int32)
    csum = jnp.cumsum(onehot, axis=0)
    cnt = csum[-1]
    offs = jnp.cumsum(cnt) - cnt
    pos = jnp.sum(onehot * (csum - 1 + offs[None, :]), axis=1).astype(jnp.int32)
    n_tiles = r // MOE_ROWS
    n_items = n_tiles + N_EXPERTS - 1
    t_start = offs // MOE_ROWS
    t_end = (offs + cnt + MOE_ROWS - 1) // MOE_ROWS
    n_e = jnp.where(cnt > 0, t_end - t_start, 0)
    item_end = jnp.cumsum(n_e)
    item_start = item_end - n_e
    total = item_end[-1]
    w = jnp.arange(n_items, dtype=jnp.int32)
    e_w = jnp.minimum(jnp.searchsorted(item_end, w, side="right"), N_EXPERTS - 1).astype(jnp.int32)
    tile_w = t_start[e_w] + (w - item_start[e_w])
    valid = w < total
    lo = jnp.maximum(offs[e_w], tile_w * MOE_ROWS) - tile_w * MOE_ROWS
    hi = jnp.minimum(offs[e_w] + cnt[e_w], (tile_w + 1) * MOE_ROWS) - tile_w * MOE_ROWS
    last = jnp.maximum(total - 1, 0)
    tile_w = jnp.where(valid, tile_w, tile_w[last]).astype(jnp.int32)
    e_w = jnp.where(valid, e_w, e_w[last]).astype(jnp.int32)
    lo = jnp.where(valid, lo, 0).astype(jnp.int32)
    hi = jnp.where(valid, hi, 0).astype(jnp.int32)
    prev = jnp.concatenate([jnp.full((1,), -1, jnp.int32), tile_w[:-1]])
    first = (valid & (tile_w != prev)).astype(jnp.int32)
    return pos, (tile_w, e_w, lo, hi, first)


def moe_ln(x, idx, wt, w_gu, w_down, g, b):
    pos, meta = _moe_plan(idx)
    xs = moe_dispatch(x, pos)
    y = moe_experts(xs, meta, w_gu, w_down)
    return moe_combine_ln(y, pos, wt, x, g, b)


def _j_masked_softmax(logits, mask):
    logits = jnp.where(mask, logits.astype(jnp.float32), NEG)
    m = jnp.max(logits, -1, keepdims=True)
    p = jnp.exp(logits - m) * mask
    return p / jnp.maximum(jnp.sum(p, -1, keepdims=True), 1e-30)


def _j_gather_pages(pool, slot, page_table):
    g = pool[slot, page_table]
    return g.reshape((g.shape[0], g.shape[1] * g.shape[2]) + g.shape[3:])


def _j_fox_project(x, w_in, b_f):
    z = x @ w_in
    d = N_HEADS * HEAD_DIM
    q, k, v, f = jnp.split(z, [d, 2 * d, 3 * d], axis=-1)
    shp = x.shape[:2] + (N_HEADS, HEAD_DIM)
    logf = jax.nn.log_sigmoid((f + b_f).astype(jnp.float32))
    return q.reshape(shp), k.reshape(shp), v.reshape(shp), logf


def _j_fox_attend(q, c_q, t_q, k, v, c_k, t_k):
    s = jnp.einsum('bqhd,bkhd->bhqk', q, k).astype(jnp.float32) * ATTN_SCALE
    s = s + jnp.transpose(c_q, (0, 2, 1))[..., :, None] - jnp.transpose(c_k, (0, 2, 1))[..., None, :]
    p = _j_masked_softmax(s, t_k[None, :] <= t_q[:, None])
    return jnp.einsum('bhqk,bkhd->bqhd', p.astype(v.dtype), v)


def _j_fox_prompt(x, w_in, b_f, w_out):
    B, T, _ = x.shape
    q, k, v, logf = _j_fox_project(x, w_in, b_f)
    c = jnp.cumsum(logf, axis=1)
    t = jnp.arange(T)

    def blk(i):
        q0 = i * Q_BLOCK
        qb = lax.dynamic_slice_in_dim(q, q0, Q_BLOCK, 1)
        cb = lax.dynamic_slice_in_dim(c, q0, Q_BLOCK, 1)
        return _j_fox_attend(qb, cb, q0 + jnp.arange(Q_BLOCK), k, v, c, t)

    o = lax.map(blk, jnp.arange(T // Q_BLOCK))
    o = jnp.moveaxis(o, 0, 1).reshape(B, T, -1)
    return o, jnp.stack([k, v], axis=2), logf


def _j_fox_sample(x, kv_pool, logf_pool, slot, page_table, w_in, b_f, w_out):
    B, T, _ = x.shape
    q, k, v, logf = _j_fox_project(x, w_in, b_f)
    kv_past = _j_gather_pages(kv_pool, slot, page_table)
    lf_past = _j_gather_pages(logf_pool, slot, page_table).astype(jnp.float32)
    P = kv_past.shape[1]
    k_all = jnp.concatenate([kv_past[:, :, 0].astype(k.dtype), k], 1)
    v_all = jnp.concatenate([kv_past[:, :, 1].astype(v.dtype), v], 1)
    c_all = jnp.cumsum(jnp.concatenate([lf_past, logf], 1), axis=1)
    t_all = jnp.arange(P + T)
    o = _j_fox_attend(q, c_all[:, P:], t_all[P:], k_all, v_all, c_all, t_all)
    return o.reshape(B, T, -1), jnp.stack([k, v], axis=2), logf


def _j_rg_mixer(x, conv_buf, h0, w_in, conv_w, conv_b, ga_w, ga_b, gx_w, gx_b, lam, w_out):
    T = x.shape[1]
    gate_branch, u = jnp.split(x @ w_in, 2, axis=-1)
    up = jnp.concatenate([conv_buf.astype(u.dtype), u], 1)
    cx = conv_b + sum(up[:, j:j + T] * conv_w[j] for j in range(CONV_W))
    cb = cx.reshape(cx.shape[:2] + (RG_BLOCKS, RG_BS))
    r = jax.nn.sigmoid(jnp.einsum('btni,nij->btnj', cb, ga_w).reshape(cx.shape) + ga_b)
    i_g = jax.nn.sigmoid(jnp.einsum('btni,nij->btnj', cb, gx_w).reshape(cx.shape) + gx_b)
    log_a = -RG_C * r.astype(jnp.float32) * jax.nn.softplus(-lam.astype(jnp.float32))
    a = jnp.exp(log_a)
    b = jnp.sqrt(-jnp.expm1(2.0 * log_a)) * (i_g * cx).astype(jnp.float32)

    def step(h, ab):
        h = ab[0] * h + ab[1]
        return h, h

    hT, hs = lax.scan(step, h0.astype(jnp.float32), (jnp.swapaxes(a, 0, 1), jnp.swapaxes(b, 0, 1)))
    hs = jnp.swapaxes(hs, 0, 1).astype(x.dtype)
    y = jax.nn.gelu(gate_branch) * hs
    return y, hT.astype(x.dtype), up[:, up.shape[1] - (CONV_W - 1):]


def _j_nsa_project(x, w_in, b_gate):
    B, T = x.shape[:2]
    q, kv, g = jnp.split(x @ w_in, [D_MODEL, D_MODEL + 6 * KV_W], axis=-1)
    q = q.reshape(B, T, NSA_G, NSA_HPG, HEAD_DIM)
    kv = kv.reshape(B, T, 6, NSA_G, HEAD_DIM)
    g = jax.nn.sigmoid(g + b_gate).reshape(B, T, 3, NSA_G, NSA_HPG)
    return q, kv, g


def _j_compress_rows(xr, pe, w1, b1, w2):
    B, L = xr.shape[:2]
    r = CMP_BLOCK // CMP_STRIDE
    n16 = L // CMP_STRIDE
    n_cmp = n16 - r + 1
    ch = xr[:, :n16 * CMP_STRIDE].reshape(B, n16, CMP_STRIDE, NSA_G, HEAD_DIM)
    blocks = jnp.concatenate([ch[:, m:m + n_cmp] for m in range(r)], axis=2) + pe[:, None, :]
    flat = jnp.swapaxes(blocks, 2, 3).reshape(B, n_cmp, NSA_G, CMP_BLOCK * HEAD_DIM)
    return jax.nn.gelu(flat @ w1 + b1) @ w2


def _j_select_blocks(p_cmp, t_q, n_sel):
    n_cmp = p_cmp.shape[-1]
    c0 = jnp.arange(n_cmp) * CMP_STRIDE
    s0 = jnp.arange(n_sel) * SEL_BLOCK
    overlap = ((c0[:, None] < s0[None, :] + SEL_BLOCK) & (c0[:, None] + CMP_BLOCK > s0[None, :])).astype(jnp.float32)
    imp = jnp.einsum('bqghn,ns->bgqs', p_cmp, overlap)
    cur = (t_q // SEL_BLOCK)[:, None]
    blk = jnp.arange(n_sel)[None, :]
    forced = (blk == 0) | (cur - blk < N_LOCAL_BLOCKS)
    score = jnp.where(blk <= cur, imp + jnp.where(forced, FORCE_BONUS, 0.0), NEG)
    top, idx = lax.top_k(score, N_SELECT)
    return idx, top > 0.5 * NEG


def _j_sel_attend(q, t_q, idx, valid, ks, vs, q_chunk):
    B, Tq = q.shape[:2]
    G = NSA_G
    nsb = ks.shape[1] // SEL_BLOCK
    kb = jnp.moveaxis(ks.reshape(B, nsb, SEL_BLOCK, G, HEAD_DIM), 3, 1)
    vb = jnp.moveaxis(vs.reshape(B, nsb, SEL_BLOCK, G, HEAD_DIM), 3, 1)
    nc = Tq // q_chunk
    n = N_SELECT * SEL_BLOCK
    qs = jnp.moveaxis(q.reshape((B, nc, q_chunk) + q.shape[2:]), 1, 0)
    ts = t_q.reshape(nc, q_chunk)
    ids = jnp.moveaxis(idx.reshape(B, G, nc, q_chunk, N_SELECT), 2, 0)
    vls = jnp.moveaxis(valid.reshape(B, G, nc, q_chunk, N_SELECT), 2, 0)
    take = jax.vmap(jax.vmap(lambda blocks, ix: blocks[ix]))

    def chunk(args):
        qc, tc, ic, vc = args
        gk = take(kb, ic).reshape(B, G, q_chunk, n, HEAD_DIM)
        gv = take(vb, ic).reshape(B, G, q_chunk, n, HEAD_DIM)
        pos = (ic[..., None] * SEL_BLOCK + jnp.arange(SEL_BLOCK)).reshape(B, G, q_chunk, n)
        ok = (pos <= tc[None, None, :, None]) & jnp.repeat(vc, SEL_BLOCK, axis=-1)
        s = jnp.einsum('bqghd,bgqnd->bgqhn', qc, gk).astype(jnp.float32) * ATTN_SCALE
        p = _j_masked_softmax(s, ok[:, :, :, None, :])
        return jnp.einsum('bgqhn,bgqnd->bqghd', p.astype(gv.dtype), gv)

    o = lax.map(chunk, (qs, ts, ids, vls))
    return jnp.moveaxis(o, 0, 1).reshape(q.shape)


def _j_window_attend(q, t_q, k, v, t_k):
    s = jnp.einsum('bqghd,bkgd->bqghk', q, k).astype(jnp.float32) * ATTN_SCALE
    dt = t_q[:, None] - t_k[None, :]
    ok = (t_k[None, :] >= 0) & (dt >= 0) & (dt <= WINDOW)
    p = _j_masked_softmax(s, ok[:, None, None, :])
    return jnp.einsum('bqghk,bkgd->bqghd', p.astype(v.dtype), v)


def _j_window_prompt(q, kw, vw):
    T = q.shape[1]
    pad = ((0, 0), (WINDOW, 0), (0, 0), (0, 0))
    kp, vp = jnp.pad(kw, pad), jnp.pad(vw, pad)
    span = WINDOW + Q_BLOCK

    def blk(i):
        q0 = i * Q_BLOCK
        qb = lax.dynamic_slice_in_dim(q, q0, Q_BLOCK, 1)
        kb = lax.dynamic_slice_in_dim(kp, q0, span, 1)
        vb = lax.dynamic_slice_in_dim(vp, q0, span, 1)
        return _j_window_attend(qb, q0 + jnp.arange(Q_BLOCK), kb, vb, q0 - WINDOW + jnp.arange(span))

    o = lax.map(blk, jnp.arange(T // Q_BLOCK))
    return jnp.moveaxis(o, 0, 1).reshape(q.shape)


def _j_cmp_and_sel(q, t_q, kv4, q_chunk, cmp_pe, cmp_w1, cmp_b1, cmp_w2):
    L = kv4.shape[1]
    kc = _j_compress_rows(kv4[:, :, 0], cmp_pe[0], cmp_w1[0], cmp_b1[0], cmp_w2[0])
    vc = _j_compress_rows(kv4[:, :, 1], cmp_pe[1], cmp_w1[1], cmp_b1[1], cmp_w2[1])
    n_cmp = kc.shape[1]
    s = jnp.einsum('bqghd,bngd->bqghn', q, kc).astype(jnp.float32) * ATTN_SCALE
    end = jnp.arange(n_cmp) * CMP_STRIDE + CMP_BLOCK - 1
    p = _j_masked_softmax(s, (end[None, :] <= t_q[:, None])[:, None, None, :])
    o_c = jnp.einsum('bqghn,bngd->bqghd', p.astype(vc.dtype), vc)
    n_sel = max(-(-L // SEL_BLOCK), N_SELECT)
    idx, valid = _j_select_blocks(p, t_q, n_sel)
    pad = ((0, 0), (0, n_sel * SEL_BLOCK - L), (0, 0), (0, 0))
    o_s = _j_sel_attend(q, t_q, idx, valid, jnp.pad(kv4[:, :, 2], pad), jnp.pad(kv4[:, :, 3], pad), q_chunk)
    return o_c, o_s


def _j_nsa_merge(o_c, o_s, o_w, g):
    y = g[:, :, 0, ..., None] * o_c + g[:, :, 1, ..., None] * o_s + g[:, :, 2, ..., None] * o_w
    return y.reshape(y.shape[:2] + (-1,))


def _j_nsa_prompt(x, w_in, b_gate, cmp_pe, cmp_w1, cmp_b1, cmp_w2, w_out):
    T = x.shape[1]
    q, kv, g = _j_nsa_project(x, w_in, b_gate)
    o_c, o_s = _j_cmp_and_sel(q, jnp.arange(T), kv[:, :, :4], SEL_Q_BLOCK, cmp_pe, cmp_w1, cmp_b1, cmp_w2)
    o_w = _j_window_prompt(q, kv[:, :, 4], kv[:, :, 5])
    y = _j_nsa_merge(o_c, o_s, o_w, g)
    return y, kv[:, :, :4], kv[:, T - min(WINDOW, T):, 4:]


def _j_nsa_sample(x, kv_pool, win_buf, slot, page_table, w_in, b_gate, cmp_pe, cmp_w1, cmp_b1, cmp_w2, w_out):
    T = x.shape[1]
    q, kv, g = _j_nsa_project(x, w_in, b_gate)
    kv_past = _j_gather_pages(kv_pool, slot, page_table).astype(kv.dtype)
    P = kv_past.shape[1]
    t_q = P + jnp.arange(T)
    kv4 = jnp.concatenate([kv_past, kv[:, :, :4]], 1)
    o_c, o_s = _j_cmp_and_sel(q, t_q, kv4, 1, cmp_pe, cmp_w1, cmp_b1, cmp_w2)
    W = win_buf.shape[1]
    wk = jnp.concatenate([win_buf.astype(kv.dtype), kv[:, :, 4:]], 1)
    o_w = _j_window_attend(q, t_q, wk[:, :, 0], wk[:, :, 1], P - W + jnp.arange(W + T))
    y = _j_nsa_merge(o_c, o_s, o_w, g)
    return y, kv[:, :, :4], wk[:, T:]


SEL_Q_BLOCK = 32


def _pad_rows(a, n):
    return jnp.pad(a, ((0, n - a.shape[0]),) + ((0, 0),) * (a.ndim - 1))


def kernel(x_prompt, x_sample, cache_fox_kv, cache_fox_logf, state_rg_h, state_rg_conv, cache_nsa_kv, state_nsa_win, page_table, fox_w_in, fox_b_f, fox_w_out, rg_w_in, rg_conv_w, rg_conv_b, rg_gate_a_w, rg_gate_a_b, rg_gate_x_w, rg_gate_x_b, rg_lambda, rg_w_out, nsa_w_in, nsa_b_gate, nsa_cmp_pe, nsa_cmp_w1, nsa_cmp_b1, nsa_cmp_w2, nsa_w_out, ffn_w_gu, ffn_w_down, moe_router_w, moe_router_b, moe_w_gu, moe_w_down, ln_mix_g, ln_mix_b, ln_ffn_g, ln_ffn_b):
    bp, tp, d = x_prompt.shape
    bs, ts, _ = x_sample.shape
    n_p, n_s = bp * tp, bs * ts
    nt = n_p + n_s
    ntp = -(-nt // TOKEN_TILE) * TOKEN_TILE
    x = _pad_rows(jnp.concatenate([x_prompt.reshape(n_p, d), x_sample.reshape(n_s, d)], 0), ntp)
    fkv_p, fkv_s, flf_p, flf_s = [], [], [], []
    rh_p, rh_s, rc_p, rc_s = [], [], [], []
    nkv_p, nkv_s, nw_p, nw_s = [], [], [], []
    for i in range(DEPTH):
        kind, slot = i % 3, i // 3
        xp = x[:n_p].reshape(bp, tp, d)
        xs = x[n_p:nt].reshape(bs, ts, d)
        if kind == 0:
            fw = (fox_w_in[slot], fox_b_f[slot], None)
            mp, kv_n, lf_n = _j_fox_prompt(xp, *fw)
            fkv_p.append(kv_n)
            flf_p.append(lf_n)
            ms, kv_n, lf_n = _j_fox_sample(xs, cache_fox_kv, cache_fox_logf, slot, page_table, *fw)
            fkv_s.append(kv_n)
            flf_s.append(lf_n)
            w_out = fox_w_out[slot]
        elif kind == 1:
            rw = (rg_w_in[slot], rg_conv_w[slot], rg_conv_b[slot], rg_gate_a_w[slot], rg_gate_a_b[slot],
                  rg_gate_x_w[slot], rg_gate_x_b[slot], rg_lambda[slot], None)
            mp, h_n, c_n = _j_rg_mixer(xp, jnp.zeros((bp, CONV_W - 1, D_RNN), xp.dtype),
                                       jnp.zeros((bp, D_RNN), jnp.float32), *rw)
            rh_p.append(h_n)
            rc_p.append(c_n)
            ms, h_n, c_n = _j_rg_mixer(xs, state_rg_conv[slot], state_rg_h[slot], *rw)
            rh_s.append(h_n)
            rc_s.append(c_n)
            w_out = rg_w_out[slot]
        else:
            nw = (nsa_w_in[slot], nsa_b_gate[slot], nsa_cmp_pe[slot], nsa_cmp_w1[slot],
                  nsa_cmp_b1[slot], nsa_cmp_w2[slot], None)
            mp, kv_n, w_n = _j_nsa_prompt(xp, *nw)
            nkv_p.append(kv_n)
            nw_p.append(w_n)
            ms, kv_n, w_n = _j_nsa_sample(xs, cache_nsa_kv, state_nsa_win[slot], slot, page_table, *nw)
            nkv_s.append(kv_n)
            nw_s.append(w_n)
            w_out = nsa_w_out[slot]
        kdim = mp.shape[-1]
        kpad = -(-kdim // LANE) * LANE
        act = jnp.concatenate([mp.reshape(n_p, kdim), ms.reshape(n_s, kdim)], 0)
        act = jnp.pad(act, ((0, ntp - nt), (0, kpad - kdim))).astype(BF16)
        w_o = jnp.pad(w_out, ((0, kpad - kdim), (0, 0))).astype(BF16)
        j = i // 2
        g_m, b_m = ln_mix_g[i][None], ln_mix_b[i][None]
        g_f, b_f = ln_ffn_g[i][None], ln_ffn_b[i][None]
        if i % 2 == 0:
            x = out_ln([act], w_o, x, g_m, b_m)
            x = ffn_ln(x, ffn_w_gu[j].astype(BF16), ffn_w_down[j].astype(BF16), g_f, b_f)
        else:
            rw_p = jnp.pad(moe_router_w[j], ((0, 0), (0, LANE - N_EXPERTS)))
            rb_p = jnp.pad(moe_router_b[j], (0, LANE - N_EXPERTS))[None]
            x, idx, wt = out_ln([act], w_o, x, g_m, b_m, router=(rw_p, rb_p))
            x = moe_ln(x, idx, wt, moe_w_gu[j].astype(BF16), moe_w_down[j].astype(BF16), g_f, b_f)
    xp = x[:n_p].reshape(bp, tp, d)
    xs = x[n_p:nt].reshape(bs, ts, d)
    return (xp, xs,
            jnp.stack(fkv_p), jnp.stack(fkv_s), jnp.stack(flf_p), jnp.stack(flf_s),
            jnp.stack(rh_p), jnp.stack(rh_s), jnp.stack(rc_p), jnp.stack(rc_s),
            jnp.stack(nkv_p), jnp.stack(nkv_s), jnp.stack(nw_p), jnp.stack(nw_s))
```

```python
import functools

import jax
import jax.numpy as jnp
from jax import lax
from jax.experimental import pallas as pl
from jax.experimental.pallas import tpu as pltpu

F32 = jnp.float32
BF16 = jnp.bfloat16
HI = lax.Precision.HIGHEST

D_MODEL = 1024
DEPTH = 4
N_HEADS = 16
HEAD_DIM = 64
ATTN_SCALE = HEAD_DIM ** -0.5
D_RNN = 1344
RG_BLOCKS = 16
RG_BS = 84
RG_BSP = 88
D_RNNP = RG_BLOCKS * RG_BSP
CONV_W = 4
RG_C = 8.0
NSA_G = 4
NSA_HPG = 4
KV_W = NSA_G * HEAD_DIM
CMP_BLOCK = 32
CMP_STRIDE = 16
CMP_HIDDEN = 256
SEL_BLOCK = 64
N_SELECT = 16
N_LOCAL_BLOCKS = 2
FORCE_BONUS = 1.0e4
WINDOW = 512
PAGE = 128
N_EXPERTS = 8
D_FF_EXPERT = 3584
DN_ALPHA = (2 * DEPTH) ** 0.25
LN_EPS = 1e-5
NEG = -1.0e30
LANE = 128
SUB = 8
TOKEN_TILE = 1024
ATT_T = 512
VMEM_LIMIT = 56 * 1024 * 1024


def _cparams(sem):
    return pltpu.CompilerParams(dimension_semantics=sem, vmem_limit_bytes=VMEM_LIMIT)


def _dot_nt(a, b, **kw):
    return lax.dot_general(a, b, (((1,), (1,)), ((), ())), preferred_element_type=F32, **kw)


def _log_sigmoid(x):
    return jnp.minimum(x, 0.0) - jnp.log1p(jnp.exp(-jnp.abs(x)))


def _proj_kernel(x_ref, w_ref, b_ref, *out_refs, segs):
    xb = x_ref[...].astype(BF16)
    oi = 0
    for start, width, act, outs in segs:
        z = jnp.dot(xb, w_ref[:, start:start + width], preferred_element_type=F32)
        if act is not None:
            z = z + b_ref[:, start:start + width]
            z = _log_sigmoid(z) if act == "log_sigmoid" else jax.nn.sigmoid(z)
        for dtype, scale in outs:
            out_refs[oi][...] = (z * scale if scale != 1.0 else z).astype(dtype)
            oi += 1


def proj(x, w, bias, segs, tm=512):
    nt, d = x.shape
    n = w.shape[1]
    out_shape, out_specs = [], []
    for _, width, _, outs in segs:
        for dtype, _ in outs:
            out_shape.append(jax.ShapeDtypeStruct((nt, width), dtype))
            out_specs.append(pl.BlockSpec((tm, width), lambda i: (i, 0)))
    return pl.pallas_call(
        functools.partial(_proj_kernel, segs=segs),
        grid=(nt // tm,),
        in_specs=[pl.BlockSpec((tm, d), lambda i: (i, 0)),
                  pl.BlockSpec((d, n), lambda i: (0, 0)),
                  pl.BlockSpec((1, n), lambda i: (0, 0))],
        out_specs=out_specs,
        out_shape=out_shape,
        compiler_params=_cparams(("parallel",)),
        name="proj",
    )(x, w, bias)


def _layer_norm(y, g, b):
    mu = jnp.mean(y, axis=-1, keepdims=True)
    yc = y - mu
    var = jnp.mean(yc * yc, axis=-1, keepdims=True)
    return yc * lax.rsqrt(var + LN_EPS) * g + b


def _top2(logits, n_valid):
    lane = lax.broadcasted_iota(jnp.int32, logits.shape, 1)
    l = jnp.where(lane < n_valid, logits, -jnp.inf)
    m1 = jnp.max(l, axis=-1, keepdims=True)
    i1 = jnp.min(jnp.where(l == m1, lane, LANE), axis=-1, keepdims=True)
    l2 = jnp.where(lane == i1, -jnp.inf, l)
    m2 = jnp.max(l2, axis=-1, keepdims=True)
    i2 = jnp.min(jnp.where(l2 == m2, lane, LANE), axis=-1, keepdims=True)
    return m1, i1, m2, i2, lane


def _out_ln_kernel(*refs, n_act, router):
    acts = refs[:n_act]
    w_ref, x_ref, g_ref, b_ref = refs[n_act:n_act + 4]
    rest = refs[n_act + 4:]
    a = acts[0][...]
    if n_act > 1:
        a = a.astype(F32)
        for r in acts[1:]:
            a = a + r[...].astype(F32)
        a = a.astype(BF16)
    m = jnp.dot(a, w_ref[...], preferred_element_type=F32)
    y = _layer_norm(DN_ALPHA * x_ref[...] + m, g_ref[...], b_ref[...])
    if not router:
        rest[0][...] = y
        return
    rw_ref, rb_ref, y_ref, idx_ref, wt_ref = rest
    y_ref[...] = y
    logits = jnp.dot(y, rw_ref[...], preferred_element_type=F32, precision=HI) + rb_ref[...]
    m1, i1, m2, i2, lane = _top2(logits, N_EXPERTS)
    e = jnp.exp(m2 - m1)
    den = 1.0 + e
    idx_ref[...] = jnp.where(lane == 0, i1, jnp.where(lane == 1, i2, 0))
    wt_ref[...] = jnp.where(lane == 0, 1.0 / den, jnp.where(lane == 1, e / den, 0.0))


def out_ln(acts, w, x, g, b, router=None, tm=512):
    nt, d = x.shape
    k = w.shape[0]
    row = lambda i: (i, 0)
    fix = lambda i: (0, 0)
    in_specs = [pl.BlockSpec((tm, k), row) for _ in acts]
    in_specs += [pl.BlockSpec((k, d), fix), pl.BlockSpec((tm, d), row),
                 pl.BlockSpec((1, d), fix), pl.BlockSpec((1, d), fix)]
    args = list(acts) + [w, x, g, b]
    out_shape = [jax.ShapeDtypeStruct((nt, d), F32)]
    out_specs = [pl.BlockSpec((tm, d), row)]
    if router is not None:
        in_specs += [pl.BlockSpec((d, LANE), fix), pl.BlockSpec((1, LANE), fix)]
        args += list(router)
        out_shape += [jax.ShapeDtypeStruct((nt, LANE), jnp.int32),
                      jax.ShapeDtypeStruct((nt, LANE), F32)]
        out_specs += [pl.BlockSpec((tm, LANE), row), pl.BlockSpec((tm, LANE), row)]
    res = pl.pallas_call(
        functools.partial(_out_ln_kernel, n_act=len(acts), router=router is not None),
        grid=(nt // tm,),
        in_specs=in_specs, out_specs=out_specs, out_shape=out_shape,
        compiler_params=_cparams(("parallel",)),
        name="out_ln",
    )(*args)
    return res if router is not None else res[0]


def _ffn_kernel(x_ref, wg_ref, wu_ref, wd_ref, g_ref, b_ref, o_ref, xb_ref, acc_ref):
    c = pl.program_id(1)

    @pl.when(c == 0)
    def _():
        xb_ref[...] = x_ref[...].astype(BF16)
        acc_ref[...] = jnp.zeros_like(acc_ref)

    xb = xb_ref[...]
    gt = jnp.dot(xb, wg_ref[...], preferred_element_type=F32)
    up = jnp.dot(xb, wu_ref[...], preferred_element_type=F32)
    h = (gt * jax.nn.sigmoid(gt) * up).astype(BF16)
    acc_ref[...] += jnp.dot(h, wd_ref[...], preferred_element_type=F32)

    @pl.when(c == pl.num_programs(1) - 1)
    def _():
        o_ref[...] = _layer_norm(DN_ALPHA * x_ref[...] + acc_ref[...], g_ref[...], b_ref[...])


def ffn_ln(x, w_gu, w_down, g, b, tm=1024, tf=256):
    nt, d = x.shape
    dff = w_down.shape[0]
    nc = dff // tf
    return pl.pallas_call(
        _ffn_kernel,
        grid=(nt // tm, nc),
        in_specs=[pl.BlockSpec((tm, d), lambda i, c: (i, 0)),
                  pl.BlockSpec((d, tf), lambda i, c: (0, c)),
                  pl.BlockSpec((d, tf), lambda i, c: (0, nc + c)),
                  pl.BlockSpec((tf, d), lambda i, c: (c, 0)),
                  pl.BlockSpec((1, d), lambda i, c: (0, 0)),
                  pl.BlockSpec((1, d), lambda i, c: (0, 0))],
        out_specs=pl.BlockSpec((tm, d), lambda i, c: (i, 0)),
        out_shape=jax.ShapeDtypeStruct((nt, d), F32),
        scratch_shapes=[pltpu.VMEM((tm, d), BF16), pltpu.VMEM((tm, d), F32)],
        compiler_params=_cparams(("parallel", "arbitrary")),
        name="ffn_ln",
    )(x, w_gu, w_gu, w_down, g, b)


MOE_ROWS = 512
MOE_TF = 512
ROW_DMA_UNROLL = 8


def _dispatch_kernel(pos_ref, x_ref, xs_hbm, sem, *, tm):
    base = pl.program_id(0) * (2 * tm)

    def row_copy(k):
        return pltpu.make_async_copy(x_ref.at[pl.ds(k // 2, 1)],
                                     xs_hbm.at[pl.ds(pos_ref[base + k], 1)], sem)

    def start(k, carry):
        row_copy(k).start()
        return carry

    def wait(k, carry):
        row_copy(k).wait()
        return carry

    lax.fori_loop(0, 2 * tm, start, 0, unroll=ROW_DMA_UNROLL)
    lax.fori_loop(0, 2 * tm, wait, 0, unroll=ROW_DMA_UNROLL)


def moe_dispatch(x, pos, tm=512):
    nt, d = x.shape
    return pl.pallas_call(
        functools.partial(_dispatch_kernel, tm=tm),
        grid_spec=pltpu.PrefetchScalarGridSpec(
            num_scalar_prefetch=1, grid=(nt // tm,),
            in_specs=[pl.BlockSpec((tm, d), lambda i, p: (i, 0))],
            out_specs=pl.BlockSpec(memory_space=pl.ANY),
            scratch_shapes=[pltpu.SemaphoreType.DMA(())]),
        out_shape=jax.ShapeDtypeStruct((2 * nt, d), F32),
        compiler_params=pltpu.CompilerParams(dimension_semantics=("arbitrary",),
                                             vmem_limit_bytes=VMEM_LIMIT, has_side_effects=True),
        name="moe_dispatch",
    )(pos, x)


def _moe_kernel(tile_ref, exp_ref, lo_ref, hi_ref, first_ref,
                xs_ref, wg_ref, wu_ref, wd_ref, y_ref, xb_ref, acc_ref):
    w = pl.program_id(0)
    c = pl.program_id(1)
    lo = lo_ref[w]
    hi = hi_ref[w]

    @pl.when(hi > lo)
    def _():
        @pl.when(c == 0)
        def _():
            xb_ref[...] = xs_ref[...].astype(BF16)
            acc_ref[...] = jnp.zeros_like(acc_ref)

        xb = xb_ref[...]
        gt = jnp.dot(xb, wg_ref[...], preferred_element_type=F32)
        up = jnp.dot(xb, wu_ref[...], preferred_element_type=F32)
        h = (gt * jax.nn.sigmoid(gt) * up).astype(BF16)
        acc_ref[...] += jnp.dot(h, wd_ref[...], preferred_element_type=F32)

        @pl.when(c == pl.num_programs(1) - 1)
        def _():
            rows = lax.broadcasted_iota(jnp.int32, acc_ref.shape, 0)
            mine = (rows >= lo) & (rows < hi)

            @pl.when(first_ref[w] == 1)
            def _():
                y_ref[...] = jnp.where(mine, acc_ref[...], 0.0)

            @pl.when(first_ref[w] == 0)
            def _():
                y_ref[...] = jnp.where(mine, acc_ref[...], y_ref[...])


def moe_experts(xs, meta, w_gu, w_down):
    r, d = xs.shape
    n_items = meta[0].shape[0]
    nc = D_FF_EXPERT // MOE_TF
    return pl.pallas_call(
        _moe_kernel,
        grid_spec=pltpu.PrefetchScalarGridSpec(
            num_scalar_prefetch=5, grid=(n_items, nc),
            in_specs=[
                pl.BlockSpec((MOE_ROWS, d), lambda w, c, t, e, lo, hi, f: (t[w], 0)),
                pl.BlockSpec((None, d, MOE_TF), lambda w, c, t, e, lo, hi, f: (e[w], 0, c)),
                pl.BlockSpec((None, d, MOE_TF), lambda w, c, t, e, lo, hi, f: (e[w], 0, nc + c)),
                pl.BlockSpec((None, MOE_TF, d), lambda w, c, t, e, lo, hi, f: (e[w], c, 0)),
            ],
            out_specs=pl.BlockSpec((MOE_ROWS, d), lambda w, c, t, e, lo, hi, f: (t[w], 0)),
            scratch_shapes=[pltpu.VMEM((MOE_ROWS, d), BF16), pltpu.VMEM((MOE_ROWS, d), F32)]),
        out_shape=jax.ShapeDtypeStruct((r, d), F32),
        compiler_params=_cparams(("arbitrary", "arbitrary")),
        name="moe_experts",
    )(*meta, xs, w_gu, w_gu, w_down)


def _combine_kernel(pos_ref, y_hbm, wt_ref, x_ref, g_ref, b_ref, o_ref, buf_ref, sem, *, tm):
    base = pl.program_id(0) * (2 * tm)

    def row_copy(k):
        return pltpu.make_async_copy(y_hbm.at[pl.ds(pos_ref[base + k], 1)],
                                     buf_ref.at[k % 2, pl.ds(k // 2, 1)], sem)

    def start(k, carry):
        row_copy(k).start()
        return carry

    def wait(k, carry):
        row_copy(k).wait()
        return carry

    lax.fori_loop(0, 2 * tm, start, 0, unroll=ROW_DMA_UNROLL)
    lax.fori_loop(0, 2 * tm, wait, 0, unroll=ROW_DMA_UNROLL)
    wt = wt_ref[...]
    y = wt[:, 0:1] * buf_ref[0] + wt[:, 1:2] * buf_ref[1]
    o_ref[...] = _layer_norm(DN_ALPHA * x_ref[...] + y, g_ref[...], b_ref[...])


def moe_combine_ln(y, pos, wt, x, g, b, tm=512):
    nt, d = x.shape
    return pl.pallas_call(
        functools.partial(_combine_kernel, tm=tm),
        grid_spec=pltpu.PrefetchScalarGridSpec(
            num_scalar_prefetch=1, grid=(nt // tm,),
            in_specs=[pl.BlockSpec(memory_space=pl.ANY),
                      pl.BlockSpec((tm, LANE), lambda i, p: (i, 0)),
                      pl.BlockSpec((tm, d), lambda i, p: (i, 0)),
                      pl.BlockSpec((1, d), lambda i, p: (0, 0)),
                      pl.BlockSpec((1, d), lambda i, p: (0, 0))],
            out_specs=pl.BlockSpec((tm, d), lambda i, p: (i, 0)),
            scratch_shapes=[pltpu.VMEM((2, tm, d), F32), pltpu.SemaphoreType.DMA(())]),
        out_shape=jax.ShapeDtypeStruct((nt, d), F32),
        compiler_params=_cparams(("arbitrary",)),
        name="moe_combine_ln",
    )(pos, y, wt, x, g, b)


def _moe_plan(idx):
    nt = idx.shape[0]
    r = 2 * nt
    e_flat = idx[:, :2].reshape(r)
    experts = jnp.arange(N_EXPERTS, dtype=jnp.int32)
    onehot = (e_flat[:, None] == experts[None, :]).astype(jnp.int32)
    csum = jnp.cumsum(onehot, axis=0)
    cnt = csum[-1]
    offs = jnp.cumsum(cnt) - cnt
    pos = jnp.sum(onehot * (csum - 1 + offs[None, :]), axis=1).astype(jnp.int32)
    n_tiles = r // MOE_ROWS
    n_items = n_tiles + N_EXPERTS - 1
    t_start = offs // MOE_ROWS
    t_end = (offs + cnt + MOE_ROWS - 1) // MOE_ROWS
    n_e = jnp.where(cnt > 0, t_end - t_start, 0)
    item_end = jnp.cumsum(n_e)
    item_start = item_end - n_e
    total = item_end[-1]
    w = jnp.arange(n_items, dtype=jnp.int32)
    e_w = jnp.minimum(jnp.sum((item_end[None, :] <= w[:, None]).astype(jnp.int32), axis=1),
                      N_EXPERTS - 1)
    tile_w = t_start[e_w] + (w - item_start[e_w])
    valid = w < total
    lo = jnp.maximum(offs[e_w], tile_w * MOE_ROWS) - tile_w * MOE_ROWS
    hi = jnp.minimum(offs[e_w] + cnt[e_w], (tile_w + 1) * MOE_ROWS) - tile_w * MOE_ROWS
    last = jnp.maximum(total - 1, 0)
    tile_w = jnp.where(valid, tile_w, tile_w[last]).astype(jnp.int32)
    e_w = jnp.where(valid, e_w, e_w[last]).astype(jnp.int32)
    lo = jnp.where(valid, lo, 0).astype(jnp.int32)
    hi = jnp.where(valid, hi, 0).astype(jnp.int32)
    prev = jnp.concatenate([jnp.full((1,), -1, jnp.int32), tile_w[:-1]])
    first = (valid & (tile_w != prev)).astype(jnp.int32)
    return pos, (tile_w, e_w, lo, hi, first)


def moe_ln(x, idx, wt, w_gu, w_down, g, b):
    pos, meta = _moe_plan(idx)
    xs = moe_dispatch(x, pos)
    y = moe_experts(xs, meta, w_gu, w_down)
    return moe_combine_ln(y, pos, wt, x, g, b)


def _gate_pair(g_ref, col0, lane):
    sel = lax.broadcasted_iota(jnp.int32, (LANE, LANE), 0) == (
        col0 + (lax.broadcasted_iota(jnp.int32, (LANE, LANE), 1) >= HEAD_DIM).astype(jnp.int32))
    return jnp.dot(g_ref[...], sel.astype(F32), preferred_element_type=F32, precision=HI)


def _softmax_step(carry, s, valid, v):
    m, l, acc = carry
    if valid is not None:
        s = jnp.where(valid, s, NEG)
    m_new = jnp.maximum(m, jnp.max(s, axis=1, keepdims=True))
    p = jnp.exp(s - m_new)
    if valid is not None:
        p = jnp.where(valid, p, 0.0)
    alpha = jnp.exp(m - m_new)
    l = alpha * l + jnp.sum(p, axis=1, keepdims=True)
    acc = alpha * acc + jnp.dot(p.astype(BF16), v, preferred_element_type=F32)
    return m_new, l, acc


def _pair_attn_kernel(*refs, mode, t):
    if mode == "fox":
        q_ref, k_ref, v_ref, c_ref, o_ref = refs
    elif mode == "sel":
        q_ref, k_ref, v_ref, sm_ref, e_ref, g_ref, o_ref = refs
    else:
        q_ref, k_ref, v_ref, g_ref, o_ref = refs
    j = pl.program_id(1)
    qi = pl.program_id(2)
    q = q_ref[...]
    lane = lax.broadcasted_iota(jnp.int32, (t, LANE), 1)
    row = lax.broadcasted_iota(jnp.int32, (t, t), 0)
    col = lax.broadcasted_iota(jnp.int32, (t, t), 1)
    causal = col <= row
    sm = sm_ref[...] if mode == "sel" else None
    halves = []
    for half in range(2):
        qh = jnp.where((lane < HEAD_DIM) == (half == 0), q, jnp.zeros_like(q))
        if mode == "fox":
            cq_row = c_ref[half, pl.ds(qi, 1), :]
            cq = jnp.sum(jnp.where(row == col, cq_row, 0.0), axis=1, keepdims=True)

        def step(kb, carry, kind):
            k0 = pl.multiple_of(kb * t, t)
            k = k_ref[pl.ds(k0, t), :]
            v = v_ref[pl.ds(k0, t), :]
            s = _dot_nt(qh, k)
            valid = None
            if mode == "fox":
                s = s + (cq - c_ref[half, pl.ds(kb, 1), :])
            if mode == "sel":
                valid = jnp.dot(sm, e_ref[kb], preferred_element_type=F32) > 0.5
            if kind == "diag":
                valid = causal if valid is None else (valid & causal)
            if kind == "band":
                valid = (col >= row) & (qi > 0)
            return _softmax_step(carry, s, valid, v)

        carry = (jnp.full((t, 1), NEG, F32), jnp.zeros((t, 1), F32), jnp.zeros((t, LANE), F32))
        if mode == "win":
            carry = step(jnp.maximum(qi - 1, 0), carry, "band")
        else:
            carry = lax.fori_loop(0, qi, lambda kb, c: step(kb, c, "full"), carry)
        m, l, acc = step(qi, carry, "diag")
        halves.append(acc / jnp.maximum(l, 1e-30))
    o = jnp.where(lane < HEAD_DIM, halves[0], halves[1])
    if mode != "fox":
        branch = 1 if mode == "sel" else 2
        o = o * _gate_pair(g_ref, branch * N_HEADS + 2 * j, lane)
    o_ref[...] = o.astype(o_ref.dtype)


def pair_attention(mode, q, k, v, kcol, vcol, bsz, seq, extra):
    t = ATT_T
    nq = seq // t
    n_pairs = N_HEADS // 2
    in_specs = [pl.BlockSpec((t, LANE), lambda b, j, i: (b * nq + i, j)),
                pl.BlockSpec((seq, LANE), lambda b, j, i: (b, kcol(j))),
                pl.BlockSpec((seq, LANE), lambda b, j, i: (b, vcol(j)))]
    if mode == "fox":
        in_specs.append(pl.BlockSpec((None, None, 2, nq, t), lambda b, j, i: (b, j, 0, 0, 0)))
    elif mode == "sel":
        in_specs += [pl.BlockSpec((t, LANE), lambda b, j, i: (b * nq + i, 0)),
                     pl.BlockSpec((None, nq, LANE, t), lambda b, j, i: (j // 2, 0, 0, 0)),
                     pl.BlockSpec((t, LANE), lambda b, j, i: (b * nq + i, 0))]
    else:
        in_specs.append(pl.BlockSpec((t, LANE), lambda b, j, i: (b * nq + i, 0)))
    return pl.pallas_call(
        functools.partial(_pair_attn_kernel, mode=mode, t=t),
        grid=(bsz, n_pairs, nq),
        in_specs=in_specs,
        out_specs=pl.BlockSpec((t, LANE), lambda b, j, i: (b * nq + i, j)),
        out_shape=jax.ShapeDtypeStruct((bsz * seq, D_MODEL), BF16),
        compiler_params=_cparams(("parallel", "parallel", "arbitrary")),
        name="attn_" + mode,
    )(q, k, v, *extra)


def _gelu(x):
    return 0.5 * x * (1.0 + jnp.tanh(0.7978845608028654 * (x + 0.044715 * (x * x * x))))


def _compress_kernel(x0_ref, x1_ref, x2_ref, x3_ref, w1_ref, b1_ref, pe_ref, w2d_ref, w2s_ref,
                     dup_ref, nat_ref):
    xs = (x0_ref, x1_ref, x2_ref, x3_ref)
    n_chunk = x0_ref.shape[0] // CMP_STRIDE
    n_rp = CMP_STRIDE // 2
    for which in range(2):
        nat = jnp.zeros((n_chunk, KV_W), F32)
        for cb in range(2):
            x_ref = xs[2 * which + cb]
            top = jnp.zeros((n_chunk, 2 * CMP_HIDDEN), F32)
            bot = jnp.zeros((n_chunk, 2 * CMP_HIDDEN), F32)
            for rp in range(n_rp):
                lhs = jnp.concatenate(
                    [x_ref[pl.ds(2 * rp, n_chunk, stride=CMP_STRIDE), :],
                     x_ref[pl.ds(2 * rp + 1, n_chunk, stride=CMP_STRIDE), :]], axis=1)
                i_top = which * 2 * n_rp + rp
                i_bot = i_top + n_rp
                top = top + jnp.dot((lhs + pe_ref[i_top]).astype(BF16), w1_ref[i_top],
                                    preferred_element_type=F32)
                bot = bot + jnp.dot((lhs + pe_ref[i_bot]).astype(BF16), w1_ref[i_bot],
                                    preferred_element_type=F32)
            hid = _gelu(top + pltpu.roll(bot, n_chunk - 1, 0) + b1_ref[which]).astype(BF16)
            for gg in range(2):
                g = 2 * cb + gg
                h = hid[:, gg * CMP_HIDDEN:(gg + 1) * CMP_HIDDEN]
                dup_ref[which, g] = jnp.dot(h, w2d_ref[which],
                                            preferred_element_type=F32).astype(dup_ref.dtype)
                nat = nat + jnp.dot(h, w2s_ref[which, g], preferred_element_type=F32)
        nat_ref[which] = nat.astype(nat_ref.dtype)


def compress(x, n_seq, seq, w1, b1, pe, w2d, w2s):
    n_chunk = seq // CMP_STRIDE
    fix = lambda s: (0, 0, 0)
    col = lambda c: pl.BlockSpec((seq, LANE), lambda s: (s, c))
    return pl.pallas_call(
        _compress_kernel,
        grid=(n_seq,),
        in_specs=[col(0), col(1), col(2), col(3),
                  pl.BlockSpec(w1.shape, fix), pl.BlockSpec(b1.shape, fix),
                  pl.BlockSpec(pe.shape, fix), pl.BlockSpec(w2d.shape, fix),
                  pl.BlockSpec(w2s.shape, lambda s: (0, 0, 0, 0))],
        out_specs=[pl.BlockSpec((None, 2, NSA_G, n_chunk, LANE), lambda s: (s, 0, 0, 0, 0)),
                   pl.BlockSpec((None, 2, n_chunk, KV_W), lambda s: (s, 0, 0, 0))],
        out_shape=[jax.ShapeDtypeStruct((n_seq, 2, NSA_G, n_chunk, LANE), BF16),
                   jax.ShapeDtypeStruct((n_seq, 2, n_chunk, KV_W), BF16)],
        compiler_params=_cparams(("parallel",)),
        name="nsa_compress",
    )(x, x, x, x, w1, b1, pe, w2d, w2s)


def _block_rank(score, blk, group):
    rank = jnp.zeros(score.shape, jnp.int32)
    for k in range(1, group):
        if group == LANE:
            other = pltpu.roll(score, k, 1)
            io = jnp.where(blk >= k, blk - k, blk - k + group)
        else:
            wrap = blk >= k
            other = jnp.where(wrap, pltpu.roll(score, k, 1), pltpu.roll(score, k + LANE - group, 1))
            io = jnp.where(wrap, blk - k, blk - k + group)
        beats = (other > score) | ((other == score) & (io < blk))
        rank = rank + beats.astype(jnp.int32)
    return rank


def _cmp_prompt_kernel(q_ref, kc_ref, vc_ref, g_ref, ov_ref, oc_ref, sm_ref, *, t):
    qi = pl.program_id(1)
    lane = lax.broadcasted_iota(jnp.int32, (t, LANE), 1)
    pos = qi * t + lax.broadcasted_iota(jnp.int32, (t, LANE), 0)
    n_cmp = kc_ref.shape[1] - 1
    valid = (lane * CMP_STRIDE + CMP_BLOCK - 1 <= pos) & (lane < n_cmp)
    gates = g_ref[...]
    imp = jnp.zeros((t, LANE), F32)
    for g in range(NSA_G):
        kc = kc_ref[g]
        vc = vc_ref[g]
        psum = jnp.zeros((t, LANE), F32)
        for pr in range(2):
            pair = 2 * g + pr
            q = q_ref[:, pair * LANE:(pair + 1) * LANE]
            outs = []
            for half in range(2):
                qh = jnp.where((lane < HEAD_DIM) == (half == 0), q, jnp.zeros_like(q))
                s = jnp.where(valid, _dot_nt(qh, kc), NEG)
                m = jnp.max(s, axis=1, keepdims=True)
                p = jnp.where(valid, jnp.exp(s - m), 0.0)
                p = p / jnp.maximum(jnp.sum(p, axis=1, keepdims=True), 1e-30)
                psum = psum + p
                outs.append(jnp.dot(p.astype(BF16), vc, preferred_element_type=F32))
            h0 = 2 * pair
            gate = jnp.where(lane < HEAD_DIM, gates[:, h0:h0 + 1], gates[:, h0 + 1:h0 + 2])
            o = jnp.where(lane < HEAD_DIM, outs[0], outs[1]) * gate
            oc_ref[:, pair * LANE:(pair + 1) * LANE] = o.astype(oc_ref.dtype)
        imp = imp + jnp.dot(psum, ov_ref[g], preferred_element_type=F32, precision=HI)
    n_blk = LANE // NSA_G
    blk = lane % n_blk
    cur = pos // SEL_BLOCK
    forced = (blk == 0) | (cur - blk < N_LOCAL_BLOCKS)
    score = jnp.where(blk <= cur, imp + jnp.where(forced, FORCE_BONUS, 0.0), NEG)
    rank = _block_rank(score, blk, n_blk)
    sm_ref[...] = jnp.where(rank < N_SELECT, 1.0, 0.0).astype(sm_ref.dtype)


def cmp_prompt(q, kc_dup, vc_dup, gates, ov, bsz, seq):
    t = ATT_T
    nq = seq // t
    n_chunk = kc_dup.shape[2]
    return pl.pallas_call(
        functools.partial(_cmp_prompt_kernel, t=t),
        grid=(bsz, nq),
        in_specs=[pl.BlockSpec((t, D_MODEL), lambda b, i: (b * nq + i, 0)),
                  pl.BlockSpec((None, NSA_G, n_chunk, LANE), lambda b, i: (b, 0, 0, 0)),
                  pl.BlockSpec((None, NSA_G, n_chunk, LANE), lambda b, i: (b, 0, 0, 0)),
                  pl.BlockSpec((t, LANE), lambda b, i: (b * nq + i, 0)),
                  pl.BlockSpec((NSA_G, LANE, LANE), lambda b, i: (0, 0, 0))],
        out_specs=[pl.BlockSpec((t, D_MODEL), lambda b, i: (b * nq + i, 0)),
                   pl.BlockSpec((t, LANE), lambda b, i: (b * nq + i, 0))],
        out_shape=[jax.ShapeDtypeStruct((bsz * seq, D_MODEL), BF16),
                   jax.ShapeDtypeStruct((bsz * seq, LANE), BF16)],
        compiler_params=_cparams(("parallel", "arbitrary")),
        name="nsa_cmp_prompt",
    )(q, kc_dup, vc_dup, gates, ov)


def _softplus(x):
    return jnp.maximum(x, 0.0) + jnp.log1p(jnp.exp(-jnp.abs(x)))


def _rg_gates_and_scan(u3, prev3, gb, cw_ref, cb_ref, wg_ref, gab_ref, gxb_ref, lam_ref):
    nt8, _, c = u3.shape
    rows = nt8 * SUB
    rowi = lax.broadcasted_iota(jnp.int32, u3.shape, 1)

    def shifted(s):
        return jnp.where(rowi >= s, pltpu.roll(u3, s, 1), pltpu.roll(prev3, s, 1))

    cx3 = (cb_ref[...] + cw_ref[0:1, :] * shifted(3) + cw_ref[1:2, :] * shifted(2)
           + cw_ref[2:3, :] * shifted(1) + cw_ref[3:4, :] * u3)
    cx = cx3.reshape(rows, c)
    z = jnp.dot(cx.astype(BF16), wg_ref[...], preferred_element_type=F32)
    r = jax.nn.sigmoid(z[:, :c] + gab_ref[...])
    ig = jax.nn.sigmoid(z[:, c:] + gxb_ref[...])
    log_a = -RG_C * r * _softplus(-lam_ref[...])
    a = jnp.exp(log_a)
    b = jnp.sqrt(1.0 - jnp.exp(2.0 * log_a)) * (ig * cx)
    a3 = a.reshape(nt8, SUB, c)
    b3 = b.reshape(nt8, SUB, c)
    for s in (1, 2, 4):
        keep = rowi >= s
        b3 = jnp.where(keep, a3 * pltpu.roll(b3, s, 1) + b3, b3)
        a3 = jnp.where(keep, a3 * pltpu.roll(a3, s, 1), a3)
    return a3, b3, _gelu(gb)


def _rg_prompt_kernel(gb_ref, u_ref, cw_ref, cb_ref, wg_ref, gab_ref, gxb_ref, lam_ref,
                      y_ref, hl_ref, cu_ref, ch_ref, a_ref, b_ref, h_ref):
    rows, c = u_ref.shape
    nt8 = rows // SUB

    @pl.when(pl.program_id(1) == 0)
    def _():
        cu_ref[...] = jnp.zeros_like(cu_ref)
        ch_ref[...] = jnp.zeros_like(ch_ref)

    u3 = u_ref[...].reshape(nt8, SUB, c)
    prev3 = jnp.concatenate([cu_ref[...][None], u3[:nt8 - 1]], axis=0)
    a3, b3, gelu_gb = _rg_gates_and_scan(u3, prev3, gb_ref[...], cw_ref, cb_ref, wg_ref,
                                         gab_ref, gxb_ref, lam_ref)
    a_ref[...] = a3
    b_ref[...] = b3

    def body(j, hb):
        h = a_ref[j] * hb + b_ref[j]
        h_ref[j] = h
        return jnp.broadcast_to(h[SUB - 1:SUB, :], (SUB, c))

    hb = lax.fori_loop(0, nt8, body, ch_ref[...])
    ch_ref[...] = hb
    cu_ref[...] = u3[nt8 - 1]
    hl_ref[...] = hb
    y_ref[...] = (gelu_gb * h_ref[...].reshape(rows, c)).astype(y_ref.dtype)


def _rg_sample_kernel(gb_ref, u_ref, prev_ref, h0_ref, cw_ref, cb_ref, wg_ref, gab_ref, gxb_ref,
                      lam_ref, y_ref, h_ref):
    rows, c = u_ref.shape
    nt8 = rows // SUB
    u3 = u_ref[...].reshape(nt8, SUB, c)
    prev3 = prev_ref[...].reshape(nt8, SUB, c)
    a3, b3, gelu_gb = _rg_gates_and_scan(u3, prev3, gb_ref[...], cw_ref, cb_ref, wg_ref,
                                         gab_ref, gxb_ref, lam_ref)
    h = (a3 * h0_ref[...].reshape(nt8, SUB, c) + b3).reshape(rows, c)
    h_ref[...] = h
    y_ref[...] = (gelu_gb * h).astype(y_ref.dtype)


def _rg_weight_specs(n_grid):
    c = D_RNNP
    fix = (lambda b, i: (0, 0)) if n_grid == 2 else (lambda i: (0, 0))
    return [pl.BlockSpec((CONV_W, c), fix), pl.BlockSpec((1, c), fix),
            pl.BlockSpec((c, 2 * c), fix), pl.BlockSpec((1, c), fix),
            pl.BlockSpec((1, c), fix), pl.BlockSpec((1, c), fix)]


def rg_prompt(gb, u, weights, bsz, seq, rows=256):
    c = D_RNNP
    nc = seq // rows
    blk = lambda b, i: (b * nc + i, 0)
    return pl.pallas_call(
        _rg_prompt_kernel,
        grid=(bsz, nc),
        in_specs=[pl.BlockSpec((rows, c), blk), pl.BlockSpec((rows, c), blk)] + _rg_weight_specs(2),
        out_specs=[pl.BlockSpec((rows, c), blk), pl.BlockSpec((SUB, c), lambda b, i: (b, 0))],
        out_shape=[jax.ShapeDtypeStruct((bsz * seq, c), BF16),
                   jax.ShapeDtypeStruct((bsz * SUB, c), F32)],
        scratch_shapes=[pltpu.VMEM((SUB, c), F32), pltpu.VMEM((SUB, c), F32),
                        pltpu.VMEM((rows // SUB, SUB, c), F32),
                        pltpu.VMEM((rows // SUB, SUB, c), F32),
                        pltpu.VMEM((rows // SUB, SUB, c), F32)],
        compiler_params=_cparams(("parallel", "arbitrary")),
        name="rg_prompt",
    )(gb, u, *weights)


def rg_sample(gb, u, prev, h0, weights, rows=256):
    c = D_RNNP
    n = u.shape[0]
    rows = min(rows, n)
    blk = lambda i: (i, 0)
    return pl.pallas_call(
        _rg_sample_kernel,
        grid=(n // rows,),
        in_specs=[pl.BlockSpec((rows, c), blk)] * 4 + _rg_weight_specs(1),
        out_specs=[pl.BlockSpec((rows, c), blk), pl.BlockSpec((rows, c), blk)],
        out_shape=[jax.ShapeDtypeStruct((n, c), BF16), jax.ShapeDtypeStruct((n, c), F32)],
        compiler_params=_cparams(("parallel",)),
        name="rg_sample",
    )(gb, u, prev, h0, *weights)


def _rows_of_heads(x16):
    n = x16.shape[1]
    return jnp.broadcast_to(x16[:, None, :], (N_HEADS, SUB, n)).reshape(N_HEADS * SUB, n)


def _softmax_step_t(m_ref, l_ref, acc_ref, s, valid, v_t):
    if valid is not None:
        s = jnp.where(valid, s, NEG)
    m_prev = m_ref[...]
    m_new = jnp.maximum(m_prev, jnp.max(s, axis=1, keepdims=True))
    p = jnp.exp(s - m_new)
    if valid is not None:
        p = jnp.where(valid, p, 0.0)
    alpha = jnp.exp(m_prev - m_new)
    l_ref[...] = alpha * l_ref[...] + jnp.sum(p, axis=1, keepdims=True)
    acc_ref[...] = alpha * acc_ref[...] + _dot_nt(p.astype(BF16), v_t.astype(BF16))
    m_ref[...] = m_new


def _fox_sample_kernel(pt_ref, q_ref, lfn_ref, kn_ref, vn_ref, k_ref, v_ref, lf_ref, o_ref,
                       m_ref, l_ref, acc_ref, sfx_ref, cq_ref):
    step = pl.program_id(1)
    rows = N_HEADS * SUB
    q = q_ref[...]
    row = lax.broadcasted_iota(jnp.int32, (rows, LANE), 0)
    col = lax.broadcasted_iota(jnp.int32, (rows, LANE), 1)
    qpos = row % SUB

    @pl.when(step == 0)
    def _():
        m_ref[...] = jnp.full_like(m_ref, NEG)
        l_ref[...] = jnp.zeros_like(l_ref)
        acc_ref[...] = jnp.zeros_like(acc_ref)
        sfx_ref[...] = jnp.zeros_like(sfx_ref)
        lf = jnp.concatenate([lfn_ref[...], jnp.zeros((LANE - SUB, LANE), F32)], axis=0)
        tri = (lax.broadcasted_iota(jnp.int32, (LANE, LANE), 0)
               >= lax.broadcasted_iota(jnp.int32, (LANE, LANE), 1)).astype(F32)
        cnew = jnp.dot(tri, lf, preferred_element_type=F32, precision=HI)
        head_of_row = (col == row // SUB).astype(F32)
        c_keys = _dot_nt(head_of_row, cnew, precision=HI)
        cq = jnp.sum(jnp.where(col == qpos, c_keys, 0.0), axis=1, keepdims=True)
        cq_ref[...] = cq
        s = jnp.dot(q, kn_ref[...].astype(BF16), preferred_element_type=F32) + (cq - c_keys)
        _softmax_step_t(m_ref, l_ref, acc_ref, s, col <= qpos, vn_ref[...])

    @pl.when(step > 0)
    def _():
        lf = lf_ref[...]
        later = (lax.broadcasted_iota(jnp.int32, (LANE, LANE), 0)
                 > lax.broadcasted_iota(jnp.int32, (LANE, LANE), 1)).astype(F32)
        sfx = jnp.dot(lf, later, preferred_element_type=F32, precision=HI) + sfx_ref[...]
        sfx_ref[...] = sfx_ref[...] + jnp.sum(lf, axis=1, keepdims=True)
        s = (jnp.dot(q, k_ref[...].astype(BF16), preferred_element_type=F32)
             + _rows_of_heads(sfx) + cq_ref[...])
        _softmax_step_t(m_ref, l_ref, acc_ref, s, None, v_ref[...])

    @pl.when(step == pl.num_programs(1) - 1)
    def _():
        o = acc_ref[...] / jnp.maximum(l_ref[...], 1e-30)
        o3 = o.reshape(N_HEADS, SUB, D_MODEL)
        own = (lax.broadcasted_iota(jnp.int32, o3.shape, 2) // HEAD_DIM
               == lax.broadcasted_iota(jnp.int32, o3.shape, 0))
        o_ref[...] = jnp.sum(jnp.where(own, o3, 0.0), axis=0)


def fox_sample(pt, qbd, lf_new, kv_new_t, cache_t, logf_t, slot, n_seq, n_pages):
    rows = N_HEADS * SUB

    def page(b, s, pt):
        return pt[b * n_pages + n_pages - jnp.maximum(s, 1)]

    return pl.pallas_call(
        _fox_sample_kernel,
        grid_spec=pltpu.PrefetchScalarGridSpec(
            num_scalar_prefetch=1, grid=(n_seq, n_pages + 1),
            in_specs=[
                pl.BlockSpec((None, rows, D_MODEL), lambda b, s, pt: (b, 0, 0)),
                pl.BlockSpec((SUB, LANE), lambda b, s, pt: (b, 0)),
                pl.BlockSpec((None, D_MODEL, LANE), lambda b, s, pt: (b, 0, 0)),
                pl.BlockSpec((None, D_MODEL, LANE), lambda b, s, pt: (b, 1, 0)),
                pl.BlockSpec((None, None, D_MODEL, LANE), lambda b, s, pt: (slot, page(b, s, pt), 0, 0)),
                pl.BlockSpec((None, None, D_MODEL, LANE), lambda b, s, pt: (slot, page(b, s, pt), 1, 0)),
                pl.BlockSpec((None, None, N_HEADS, LANE), lambda b, s, pt: (slot, page(b, s, pt), 0, 0)),
            ],
            out_specs=pl.BlockSpec((SUB, D_MODEL), lambda b, s, pt: (b, 0)),
            scratch_shapes=[pltpu.VMEM((rows, 1), F32), pltpu.VMEM((rows, 1), F32),
                            pltpu.VMEM((rows, D_MODEL), F32), pltpu.VMEM((N_HEADS, LANE), F32),
                            pltpu.VMEM((rows, 1), F32)]),
        out_shape=jax.ShapeDtypeStruct((n_seq * SUB, D_MODEL), F32),
        compiler_params=_cparams(("parallel", "arbitrary")),
        name="fox_sample",
    )(pt, qbd, lf_new, kv_new_t, kv_new_t, cache_t, cache_t, logf_t)


def _page_transpose_kernel(pt_ref, x_ref, o_ref):
    o_ref[...] = x_ref[...].T


def gather_cmp_pages(pt, cache_t, n_seq, n_pages):
    return pl.pallas_call(
        _page_transpose_kernel,
        grid_spec=pltpu.PrefetchScalarGridSpec(
            num_scalar_prefetch=1, grid=(n_seq, n_pages),
            in_specs=[pl.BlockSpec((None, 2 * KV_W, PAGE), lambda b, p, pt: (pt[b * n_pages + p], 0, 0))],
            out_specs=pl.BlockSpec((PAGE, 2 * KV_W), lambda b, p, pt: (b * n_pages + p, 0))),
        out_shape=jax.ShapeDtypeStruct((n_seq * n_pages * PAGE, 2 * KV_W), F32),
        compiler_params=_cparams(("parallel", "arbitrary")),
        name="nsa_gather_cmp",
    )(pt, cache_t)


def _nsa_sample_kernel(pt_ref, q_ref, g_ref, kv_ref, ov_ref, e_ref, win_ref, new_ref, pg_ref, o_ref,
                       km_ref, m_ref, l_ref, acc_ref, oc_ref, ow_ref):
    step = pl.program_id(1)
    rows = N_HEADS * SUB
    n_pages = km_ref.shape[0] - 1
    q = q_ref[...]
    row = lax.broadcasted_iota(jnp.int32, (rows, LANE), 0)
    col = lax.broadcasted_iota(jnp.int32, (rows, LANE), 1)
    qpos = row % SUB

    @pl.when(step == 0)
    def _():
        n_cmp = kv_ref.shape[1] - 1
        valid = col < n_cmp
        s = jnp.where(valid, _dot_nt(q, kv_ref[0]), NEG)
        m = jnp.max(s, axis=1, keepdims=True)
        p = jnp.where(valid, jnp.exp(s - m), 0.0)
        p = p / jnp.maximum(jnp.sum(p, axis=1, keepdims=True), 1e-30)
        oc_ref[...] = jnp.dot(p.astype(BF16), kv_ref[1], preferred_element_type=F32)
        psum = jnp.sum(p.reshape(NSA_G, NSA_HPG, SUB, LANE), axis=1).reshape(NSA_G * SUB, LANE)
        imp = jnp.dot(psum, ov_ref[...], preferred_element_type=F32, precision=HI)
        blk = lax.broadcasted_iota(jnp.int32, imp.shape, 1)
        cur = (n_pages * PAGE) // SEL_BLOCK
        forced = (blk == 0) | (cur - blk < N_LOCAL_BLOCKS)
        score = jnp.where(blk <= cur, imp + jnp.where(forced, FORCE_BONUS, 0.0), NEG)
        sel = (_block_rank(score, blk, LANE) < N_SELECT).astype(BF16)
        sel_rows = jnp.broadcast_to(sel.reshape(NSA_G, 1, SUB, LANE),
                                    (NSA_G, NSA_HPG, SUB, LANE)).reshape(rows, LANE)
        for pg in range(km_ref.shape[0]):
            km_ref[pg] = jnp.dot(sel_rows, e_ref[pg], preferred_element_type=F32)
        wrow = lax.broadcasted_iota(jnp.int32, (rows, WINDOW), 0) % SUB
        wcol = lax.broadcasted_iota(jnp.int32, (rows, WINDOW), 1)
        new_ok = col <= qpos
        sw = jnp.where(wcol >= wrow,
                       jnp.dot(q, win_ref[0:KV_W, :].astype(BF16), preferred_element_type=F32), NEG)
        sn = jnp.where(new_ok, jnp.dot(q, new_ref[2 * KV_W:3 * KV_W, :].astype(BF16),
                                       preferred_element_type=F32), NEG)
        mw = jnp.maximum(jnp.max(sw, axis=1, keepdims=True), jnp.max(sn, axis=1, keepdims=True))
        pw = jnp.where(wcol >= wrow, jnp.exp(sw - mw), 0.0)
        pn = jnp.where(new_ok, jnp.exp(sn - mw), 0.0)
        lw = jnp.sum(pw, axis=1, keepdims=True) + jnp.sum(pn, axis=1, keepdims=True)
        ow = (_dot_nt(pw.astype(BF16), win_ref[KV_W:2 * KV_W, :].astype(BF16))
              + _dot_nt(pn.astype(BF16), new_ref[3 * KV_W:4 * KV_W, :].astype(BF16)))
        ow_ref[...] = ow / jnp.maximum(lw, 1e-30)
        m_ref[...] = jnp.full_like(m_ref, NEG)
        l_ref[...] = jnp.zeros_like(l_ref)
        acc_ref[...] = jnp.zeros_like(acc_ref)
        s = jnp.dot(q, new_ref[0:KV_W, :].astype(BF16), preferred_element_type=F32)
        _softmax_step_t(m_ref, l_ref, acc_ref, s, new_ok & (km_ref[n_pages] > 0.5),
                        new_ref[KV_W:2 * KV_W, :])

    @pl.when(step > 0)
    def _():
        s = jnp.dot(q, pg_ref[0:KV_W, :].astype(BF16), preferred_element_type=F32)
        _softmax_step_t(m_ref, l_ref, acc_ref, s, km_ref[step - 1] > 0.5, pg_ref[KV_W:2 * KV_W, :])

    @pl.when(step == pl.num_programs(1) - 1)
    def _():
        g = g_ref[...]
        o_s = acc_ref[...] / jnp.maximum(l_ref[...], 1e-30)
        o_ref[...] = g[:, 0:1] * oc_ref[...] + g[:, 1:2] * o_s + g[:, 2:3] * ow_ref[...]


def nsa_sample(pt, qbd, grow, kv_nat, ov, emat, win_t, new_t, cache_t, n_seq, n_pages):
    rows = N_HEADS * SUB
    n_chunk = kv_nat.shape[2]

    def page(b, s, pt):
        return pt[b * n_pages + jnp.maximum(s, 1) - 1]

    return pl.pallas_call(
        _nsa_sample_kernel,
        grid_spec=pltpu.PrefetchScalarGridSpec(
            num_scalar_prefetch=1, grid=(n_seq, n_pages + 1),
            in_specs=[
                pl.BlockSpec((None, rows, KV_W), lambda b, s, pt: (b, 0, 0)),
                pl.BlockSpec((None, rows, LANE), lambda b, s, pt: (b, 0, 0)),
                pl.BlockSpec((None, 2, n_chunk, KV_W), lambda b, s, pt: (b, 0, 0, 0)),
                pl.BlockSpec((LANE, LANE), lambda b, s, pt: (0, 0)),
                pl.BlockSpec((n_pages + 1, LANE, LANE), lambda b, s, pt: (0, 0, 0)),
                pl.BlockSpec((None, 2 * KV_W, WINDOW), lambda b, s, pt: (b, 0, 0)),
                pl.BlockSpec((None, 4 * KV_W, LANE), lambda b, s, pt: (b, 0, 0)),
                pl.BlockSpec((None, 2 * KV_W, PAGE), lambda b, s, pt: (page(b, s, pt), 1, 0)),
            ],
            out_specs=pl.BlockSpec((None, rows, KV_W), lambda b, s, pt: (b, 0, 0)),
            scratch_shapes=[pltpu.VMEM((n_pages + 1, rows, LANE), F32),
                            pltpu.VMEM((rows, 1), F32), pltpu.VMEM((rows, 1), F32),
                            pltpu.VMEM((rows, KV_W), F32), pltpu.VMEM((rows, KV_W), F32),
                            pltpu.VMEM((rows, KV_W), F32)]),
        out_shape=jax.ShapeDtypeStruct((n_seq, rows, KV_W), F32),
        compiler_params=_cparams(("parallel", "arbitrary")),
        name="nsa_sample",
    )(pt, qbd, grow, kv_nat, ov, emat, win_t, new_t, cache_t)


def _pad_rows(a, n):
    return jnp.pad(a, ((0, n - a.shape[0]),) + ((0, 0),) * (a.ndim - 1))


def _pad_cols(a, n):
    return jnp.pad(a, ((0, 0),) * (a.ndim - 1) + ((0, n - a.shape[-1]),))


FOX_SEGS = ((0, D_MODEL, None, ((BF16, ATTN_SCALE),)),
            (D_MODEL, 2 * D_MODEL, None, ((F32, 1.0), (BF16, 1.0))),
            (3 * D_MODEL, LANE, "log_sigmoid", ((F32, 1.0),)))


def fox_mixer(x, dims, w_in, b_f, cache_kv, cache_logf, slot, pt):
    bp, tp, bs, ts, n_p, n_s = dims
    nt = n_p + n_s
    n_pages = pt.shape[0] // bs
    w = _pad_cols(w_in, 3 * D_MODEL + LANE).astype(BF16)
    bias = jnp.zeros((1, 3 * D_MODEL + LANE), F32).at[0, 3 * D_MODEL:3 * D_MODEL + N_HEADS].set(b_f)
    q, kv, kvb, logf = proj(x, w, bias, FOX_SEGS)
    nq = tp // ATT_T
    c = jnp.cumsum(logf[:n_p, :N_HEADS].reshape(bp, tp, N_HEADS), axis=1)
    c = jnp.transpose(c, (0, 2, 1)).reshape(bp, N_HEADS // 2, 2, nq, ATT_T)
    o_p = pair_attention("fox", q, kvb, kvb, lambda j: j, lambda j: N_HEADS // 2 + j, bp, tp, (c,))
    head = jnp.arange(D_MODEL, dtype=jnp.int32) // HEAD_DIM
    q_s = q[n_p:nt].reshape(bs, 1, ts, D_MODEL)
    own = (head[None, :] == jnp.arange(N_HEADS, dtype=jnp.int32)[:, None])[None, :, None, :]
    qbd = jnp.where(own, q_s, jnp.zeros_like(q_s)).reshape(bs, N_HEADS * ts, D_MODEL)
    kv_new_t = _pad_cols(jnp.transpose(kv[n_p:nt].reshape(bs, ts, 2 * D_MODEL), (0, 2, 1)), LANE)
    cache_t = jnp.transpose(cache_kv, (0, 1, 3, 4, 5, 2)).reshape(
        cache_kv.shape[0], cache_kv.shape[1], 2 * D_MODEL, PAGE)
    logf_t = jnp.transpose(cache_logf, (0, 1, 3, 2))
    o_s = fox_sample(pt, qbd, logf[n_p:nt], kv_new_t, cache_t, logf_t, slot, bs, n_pages)
    act = _pad_rows(jnp.concatenate([o_p, o_s.astype(BF16)], 0), x.shape[0])
    kv_p = kv[:n_p].reshape(bp, tp, 2, N_HEADS, HEAD_DIM)
    kv_s = kv[n_p:nt].reshape(bs, ts, 2, N_HEADS, HEAD_DIM)
    lf_p = logf[:n_p, :N_HEADS].reshape(bp, tp, N_HEADS)
    lf_s = logf[n_p:nt, :N_HEADS].reshape(bs, ts, N_HEADS)
    return act, (kv_p, kv_s, lf_p, lf_s)


def _rg_pad(a):
    shp = a.shape[:-1]
    a = a.reshape(shp + (RG_BLOCKS, RG_BS))
    return _pad_cols(a, RG_BSP).reshape(shp + (D_RNNP,))


RG_SEGS = ((0, D_RNNP, None, ((F32, 1.0),)), (D_RNNP, D_RNNP, None, ((F32, 1.0),)))


def rg_mixer(x, dims, w_in, conv_w, conv_b, ga_w, ga_b, gx_w, gx_b, lam, h0, conv_buf):
    bp, tp, bs, ts, n_p, n_s = dims
    nt = n_p + n_s
    w = jnp.concatenate([_rg_pad(w_in[:, :D_RNN]), _rg_pad(w_in[:, D_RNN:])], axis=1).astype(BF16)
    gb, u = proj(x, w, jnp.zeros((1, 2 * D_RNNP), F32), RG_SEGS)
    eye = jnp.eye(RG_BLOCKS, dtype=F32)

    def block_diag(wb):
        wp = jnp.pad(wb, ((0, 0), (0, RG_BSP - RG_BS), (0, RG_BSP - RG_BS)))
        return jnp.einsum("nij,nm->nimj", wp, eye).reshape(D_RNNP, D_RNNP)

    wg = jnp.concatenate([block_diag(ga_w), block_diag(gx_w)], axis=1).astype(BF16)
    weights = (_rg_pad(conv_w), _rg_pad(conv_b)[None], wg, _rg_pad(ga_b)[None],
               _rg_pad(gx_b)[None], _rg_pad(lam)[None])
    y_p, h_last = rg_prompt(gb[:n_p], u[:n_p], weights, bp, tp)
    prev = jnp.pad(_rg_pad(conv_buf), ((0, 0), (SUB - (CONV_W - 1), 0), (0, 0))).reshape(n_s, D_RNNP)
    h0r = jnp.broadcast_to(_rg_pad(h0)[:, None, :], (bs, ts, D_RNNP)).reshape(n_s, D_RNNP)
    y_s, h_s = rg_sample(gb[n_p:nt], u[n_p:nt], prev, h0r, weights)
    act = _pad_rows(jnp.concatenate([y_p, y_s], 0), x.shape[0])

    def unpad(a):
        return a.reshape(a.shape[:-1] + (RG_BLOCKS, RG_BSP))[..., :RG_BS].reshape(a.shape[:-1] + (D_RNN,))

    u_p = u[:n_p].reshape(bp, tp, D_RNNP)
    u_s = u[n_p:nt].reshape(bs, ts, D_RNNP)
    states = (unpad(h_last.reshape(bp, SUB, D_RNNP)[:, SUB - 1]),
              unpad(h_s.reshape(bs, ts, D_RNNP)[:, ts - 1]),
              unpad(u_p[:, tp - (CONV_W - 1):]), unpad(u_s[:, ts - (CONV_W - 1):]))
    return act, states


NSA_KV = 6 * KV_W
NSA_SEGS = ((0, D_MODEL, None, ((BF16, ATTN_SCALE),)),
            (D_MODEL, NSA_KV, None, ((F32, 1.0), (BF16, 1.0))),
            (D_MODEL + NSA_KV, LANE, "sigmoid", ((F32, 1.0),)))


def _overlap(n_sel_lanes, lane0):
    c0 = jnp.arange(LANE, dtype=jnp.int32)[:, None] * CMP_STRIDE
    lane = jnp.arange(LANE, dtype=jnp.int32)[None, :]
    s0 = (lane - lane0) * SEL_BLOCK
    hit = (c0 < s0 + SEL_BLOCK) & (c0 + CMP_BLOCK > s0) & (lane >= lane0) & (lane < lane0 + n_sel_lanes)
    return hit.astype(F32)


def _dup_groups(a):
    a = a.reshape(a.shape[0], NSA_G, 1, HEAD_DIM)
    return jnp.broadcast_to(a, (a.shape[0], NSA_G, 2, HEAD_DIM)).reshape(a.shape[0], 2 * KV_W)


def nsa_mixer(x, dims, w_in, b_gate, cmp_pe, cmp_w1, cmp_b1, cmp_w2, cache_kv, win_buf, pt):
    bp, tp, bs, ts, n_p, n_s = dims
    nt = n_p + n_s
    n_pages = pt.shape[0] // bs
    past = n_pages * PAGE
    n_w = D_MODEL + NSA_KV + LANE
    w = _pad_cols(w_in, n_w).astype(BF16)
    n_gate = 3 * N_HEADS
    bias = jnp.zeros((1, n_w), F32).at[0, D_MODEL + NSA_KV:D_MODEL + NSA_KV + n_gate].set(b_gate)
    q, kv, kvb, gates = proj(x, w, bias, NSA_SEGS)
    n_rp = CMP_STRIDE // 2
    eye2 = jnp.eye(2, dtype=F32)
    w1 = cmp_w1.reshape(2, 2, n_rp, 2, HEAD_DIM, CMP_HIDDEN)
    w1 = jnp.einsum("whpidn,gm->whpigdmn", w1, eye2)
    w1 = w1.reshape(2 * 2 * n_rp, 4 * HEAD_DIM, 2 * CMP_HIDDEN).astype(BF16)
    b1 = jnp.concatenate([cmp_b1, cmp_b1], axis=-1)[:, None, :]
    pe = jnp.broadcast_to(cmp_pe.reshape(2, 2, n_rp, 2, 1, HEAD_DIM), (2, 2, n_rp, 2, 2, HEAD_DIM))
    pe = pe.reshape(2 * 2 * n_rp, 1, 4 * HEAD_DIM)
    w2d = jnp.concatenate([cmp_w2, cmp_w2], axis=-1).astype(BF16)
    slab = jnp.eye(NSA_G, dtype=F32)
    w2s = jnp.einsum("wkd,gm->wgkmd", cmp_w2, slab).reshape(2, NSA_G, CMP_HIDDEN, KV_W).astype(BF16)
    kc_dup, _ = compress(kv, bp, tp, w1, b1, pe, w2d, w2s)
    ov_p = jnp.stack([_overlap(LANE // NSA_G, g * (LANE // NSA_G)) for g in range(NSA_G)])
    o_c, selmask = cmp_prompt(q, kc_dup[:, 0], kc_dup[:, 1], gates, ov_p, bp, tp)
    nq = tp // ATT_T
    key_blk = jnp.arange(tp, dtype=jnp.int32) // SEL_BLOCK
    lane = jnp.arange(LANE, dtype=jnp.int32)
    emat = (lane[None, :, None] == (jnp.arange(NSA_G, dtype=jnp.int32)[:, None, None] * (LANE // NSA_G)
                                    + key_blk[None, None, :])).astype(BF16)
    emat = jnp.transpose(emat.reshape(NSA_G, LANE, nq, ATT_T), (0, 2, 1, 3))
    kvb_p = kvb[:n_p]
    k_sel, v_sel, k_win, v_win = (_dup_groups(kvb_p[:, i * KV_W:(i + 1) * KV_W]) for i in range(2, 6))
    grp = lambda j: j // 2
    o_s = pair_attention("sel", q, k_sel, v_sel, grp, grp, bp, tp, (selmask, emat, gates))
    o_w = pair_attention("win", q, k_win, v_win, grp, grp, bp, tp, (gates,))
    cache_t = jnp.transpose(cache_kv, (0, 2, 3, 4, 1)).reshape(cache_kv.shape[0], 4 * KV_W, PAGE)
    x_cmp = gather_cmp_pages(pt, cache_t, bs, n_pages)
    _, kv_nat = compress(x_cmp, bs, past, w1, b1, pe, w2d, w2s)
    q_s = q[n_p:nt].reshape(bs, ts, N_HEADS, 1, HEAD_DIM)
    grp_of_head = jnp.arange(N_HEADS, dtype=jnp.int32) // NSA_HPG
    own = (grp_of_head[:, None] == jnp.arange(NSA_G, dtype=jnp.int32)[None, :])[None, None, :, :, None]
    qbd = jnp.where(own, q_s, jnp.zeros_like(q_s))
    qbd = jnp.transpose(qbd, (0, 2, 1, 3, 4)).reshape(bs, N_HEADS * ts, KV_W)
    g_s = gates[n_p:nt, :n_gate].reshape(bs, ts, 3, N_HEADS)
    grow = _pad_cols(jnp.transpose(g_s, (0, 3, 1, 2)).reshape(bs, N_HEADS * ts, 3), LANE)
    n_sel = -(-(past + ts) // SEL_BLOCK)
    ov_s = _overlap(n_sel, 0)
    keys = jnp.arange((n_pages + 1) * PAGE, dtype=jnp.int32) // SEL_BLOCK
    emat_s = (lane[:, None] == keys[None, :]).astype(BF16)
    emat_s = jnp.transpose(emat_s.reshape(LANE, n_pages + 1, PAGE), (1, 0, 2))
    win_t = jnp.transpose(win_buf, (0, 2, 3, 4, 1)).reshape(bs, 2 * KV_W, win_buf.shape[1])
    kv_s = kv[n_p:nt].reshape(bs, ts, NSA_KV)
    new_t = _pad_cols(jnp.transpose(kv_s[:, :, 2 * KV_W:], (0, 2, 1)), LANE)
    y_rows = nsa_sample(pt, qbd, grow, kv_nat, ov_s, emat_s, win_t, new_t, cache_t, bs, n_pages)
    y5 = y_rows.reshape(bs, NSA_G, NSA_HPG, ts, NSA_G, HEAD_DIM)
    y_s = jnp.stack([y5[:, g, :, :, g, :] for g in range(NSA_G)], axis=1)
    y_s = jnp.transpose(y_s, (0, 3, 1, 2, 4)).reshape(n_s, D_MODEL).astype(BF16)
    pad = x.shape[0] - n_p
    acts = [_pad_rows(jnp.concatenate([o_c, y_s], 0), x.shape[0]),
            _pad_rows(o_s, x.shape[0]), _pad_rows(o_w, x.shape[0])]
    kv_p6 = kv[:n_p].reshape(bp, tp, 6, NSA_G, HEAD_DIM)
    kv_s6 = kv_s.reshape(bs, ts, 6, NSA_G, HEAD_DIM)
    w_keep = min(WINDOW, tp)
    win_new = jnp.concatenate([win_buf, kv_s6[:, :, 4:]], axis=1)[:, ts:]
    return acts, (kv_p6[:, :, :4], kv_s6[:, :, :4], kv_p6[:, tp - w_keep:, 4:], win_new)


def kernel(x_prompt, x_sample, cache_fox_kv, cache_fox_logf, state_rg_h, state_rg_conv, cache_nsa_kv, state_nsa_win, page_table, fox_w_in, fox_b_f, fox_w_out, rg_w_in, rg_conv_w, rg_conv_b, rg_gate_a_w, rg_gate_a_b, rg_gate_x_w, rg_gate_x_b, rg_lambda, rg_w_out, nsa_w_in, nsa_b_gate, nsa_cmp_pe, nsa_cmp_w1, nsa_cmp_b1, nsa_cmp_w2, nsa_w_out, ffn_w_gu, ffn_w_down, moe_router_w, moe_router_b, moe_w_gu, moe_w_down, ln_mix_g, ln_mix_b, ln_ffn_g, ln_ffn_b):
    bp, tp, d = x_prompt.shape
    bs, ts, _ = x_sample.shape
    assert ts == SUB and tp % ATT_T == 0 and WINDOW == ATT_T
    n_p, n_s = bp * tp, bs * ts
    nt = n_p + n_s
    dims = (bp, tp, bs, ts, n_p, n_s)
    ntp = -(-nt // TOKEN_TILE) * TOKEN_TILE
    x = _pad_rows(jnp.concatenate([x_prompt.reshape(n_p, d), x_sample.reshape(n_s, d)], 0), ntp)
    pt = page_table.reshape(-1)
    fox_out, rg_out, nsa_out = [], [], []
    for i in range(DEPTH):
        kind, slot = i % 3, i // 3
        if kind == 0:
            act, new = fox_mixer(x, dims, fox_w_in[slot], fox_b_f[slot], cache_fox_kv,
                                 cache_fox_logf, slot, pt)
            fox_out.append(new)
            acts, w_out = [act], fox_w_out[slot]
        elif kind == 1:
            act, new = rg_mixer(x, dims, rg_w_in[slot], rg_conv_w[slot], rg_conv_b[slot],
                                rg_gate_a_w[slot], rg_gate_a_b[slot], rg_gate_x_w[slot],
                                rg_gate_x_b[slot], rg_lambda[slot], state_rg_h[slot],
                                state_rg_conv[slot])
            rg_out.append(new)
            acts = [act]
            w_out = jnp.pad(rg_w_out[slot].reshape(RG_BLOCKS, RG_BS, d),
                            ((0, 0), (0, RG_BSP - RG_BS), (0, 0))).reshape(D_RNNP, d)
        else:
            acts, new = nsa_mixer(x, dims, nsa_w_in[slot], nsa_b_gate[slot], nsa_cmp_pe[slot],
                                  nsa_cmp_w1[slot], nsa_cmp_b1[slot], nsa_cmp_w2[slot],
                                  cache_nsa_kv[slot], state_nsa_win[slot], pt)
            nsa_out.append(new)
            w_out = nsa_w_out[slot]
        w_o = w_out.astype(BF16)
        j = i // 2
        g_m, b_m = ln_mix_g[i][None], ln_mix_b[i][None]
        g_f, b_f = ln_ffn_g[i][None], ln_ffn_b[i][None]
        if i % 2 == 0:
            x = out_ln(acts, w_o, x, g_m, b_m)
            x = ffn_ln(x, ffn_w_gu[j].astype(BF16), ffn_w_down[j].astype(BF16), g_f, b_f)
        else:
            rw_p = _pad_cols(moe_router_w[j], LANE)
            rb_p = _pad_cols(moe_router_b[j][None], LANE)
            x, idx, wt = out_ln(acts, w_o, x, g_m, b_m, router=(rw_p, rb_p))
            x = moe_ln(x, idx, wt, moe_w_gu[j].astype(BF16), moe_w_down[j].astype(BF16), g_f, b_f)
    xp = x[:n_p].reshape(bp, tp, d)
    xs = x[n_p:nt].reshape(bs, ts, d)
    stack = lambda outs, k: jnp.stack([o[k] for o in outs])
    return (xp, xs,
            stack(fox_out, 0), stack(fox_out, 1), stack(fox_out, 2), stack(fox_out, 3),
            stack(rg_out, 0), stack(rg_out, 1), stack(rg_out, 2), stack(rg_out, 3),
            stack(nsa_out, 0), stack(nsa_out, 1), stack(nsa_out, 2), stack(nsa_out, 3))
```

```python
import functools

import jax
import jax.numpy as jnp
from jax import lax
from jax.experimental import pallas as pl
from jax.experimental.pallas import tpu as pltpu

F32 = jnp.float32
BF16 = jnp.bfloat16
HI = lax.Precision.HIGHEST

D_MODEL = 1024
DEPTH = 4
N_HEADS = 16
HEAD_DIM = 64
ATTN_SCALE = HEAD_DIM ** -0.5
D_RNN = 1344
RG_BLOCKS = 16
RG_BS = 84
RG_BSP = 88
D_RNNP = RG_BLOCKS * RG_BSP
CONV_W = 4
RG_C = 8.0
NSA_G = 4
NSA_HPG = 4
KV_W = NSA_G * HEAD_DIM
CMP_BLOCK = 32
CMP_STRIDE = 16
CMP_HIDDEN = 256
SEL_BLOCK = 64
N_SELECT = 16
N_LOCAL_BLOCKS = 2
FORCE_BONUS = 1.0e4
WINDOW = 512
PAGE = 128
N_EXPERTS = 8
D_FF_EXPERT = 3584
DN_ALPHA = (2 * DEPTH) ** 0.25
LN_EPS = 1e-5
NEG = -1.0e30
LANE = 128
SUB = 8
TOKEN_TILE = 1024
ATT_T = 512
VMEM_LIMIT = 56 * 1024 * 1024


def _cparams(sem):
    return pltpu.CompilerParams(dimension_semantics=sem, vmem_limit_bytes=VMEM_LIMIT)


def _dot_nt(a, b, **kw):
    return lax.dot_general(a, b, (((1,), (1,)), ((), ())), preferred_element_type=F32, **kw)


def _log_sigmoid(x):
    return jnp.minimum(x, 0.0) - jnp.log1p(jnp.exp(-jnp.abs(x)))


def _proj_kernel(x_ref, w_ref, b_ref, *out_refs, segs):
    xb = x_ref[...].astype(BF16)
    oi = 0
    for start, width, act, outs in segs:
        z = jnp.dot(xb, w_ref[:, start:start + width], preferred_element_type=F32)
        if act is not None:
            z = z + b_ref[:, start:start + width]
            z = _log_sigmoid(z) if act == "log_sigmoid" else jax.nn.sigmoid(z)
        for dtype, scale in outs:
            out_refs[oi][...] = (z * scale if scale != 1.0 else z).astype(dtype)
            oi += 1


def proj(x, w, bias, segs, tm=512):
    nt, d = x.shape
    n = w.shape[1]
    out_shape, out_specs = [], []
    for _, width, _, outs in segs:
        for dtype, _ in outs:
            out_shape.append(jax.ShapeDtypeStruct((nt, width), dtype))
            out_specs.append(pl.BlockSpec((tm, width), lambda i: (i, 0)))
    return pl.pallas_call(
        functools.partial(_proj_kernel, segs=segs),
        grid=(nt // tm,),
        in_specs=[pl.BlockSpec((tm, d), lambda i: (i, 0)),
                  pl.BlockSpec((d, n), lambda i: (0, 0)),
                  pl.BlockSpec((1, n), lambda i: (0, 0))],
        out_specs=out_specs,
        out_shape=out_shape,
        compiler_params=_cparams(("parallel",)),
        name="proj",
    )(x, w, bias)


def _layer_norm(y, g, b):
    mu = jnp.mean(y, axis=-1, keepdims=True)
    yc = y - mu
    var = jnp.mean(yc * yc, axis=-1, keepdims=True)
    return yc * lax.rsqrt(var + LN_EPS) * g + b


def _top2(logits, n_valid):
    lane = lax.broadcasted_iota(jnp.int32, logits.shape, 1)
    l = jnp.where(lane < n_valid, logits, -jnp.inf)
    m1 = jnp.max(l, axis=-1, keepdims=True)
    i1 = jnp.min(jnp.where(l == m1, lane, LANE), axis=-1, keepdims=True)
    l2 = jnp.where(lane == i1, -jnp.inf, l)
    m2 = jnp.max(l2, axis=-1, keepdims=True)
    i2 = jnp.min(jnp.where(l2 == m2, lane, LANE), axis=-1, keepdims=True)
    return m1, i1, m2, i2, lane


def _out_ln_kernel(*refs, n_act, router):
    acts = refs[:n_act]
    w_ref, x_ref, g_ref, b_ref = refs[n_act:n_act + 4]
    rest = refs[n_act + 4:]
    a = acts[0][...]
    if n_act > 1:
        a = a.astype(F32)
        for r in acts[1:]:
            a = a + r[...].astype(F32)
        a = a.astype(BF16)
    m = jnp.dot(a, w_ref[...], preferred_element_type=F32)
    y = _layer_norm(DN_ALPHA * x_ref[...] + m, g_ref[...], b_ref[...])
    if not router:
        rest[0][...] = y
        return
    rw_ref, rb_ref, y_ref, idx_ref, wt_ref = rest
    y_ref[...] = y
    logits = jnp.dot(y, rw_ref[...], preferred_element_type=F32, precision=HI) + rb_ref[...]
    m1, i1, m2, i2, lane = _top2(logits, N_EXPERTS)
    e = jnp.exp(m2 - m1)
    den = 1.0 + e
    idx_ref[...] = jnp.where(lane == 0, i1, jnp.where(lane == 1, i2, 0))
    wt_ref[...] = jnp.where(lane == 0, 1.0 / den, jnp.where(lane == 1, e / den, 0.0))


def out_ln(acts, w, x, g, b, router=None, tm=512):
    nt, d = x.shape
    k = w.shape[0]
    row = lambda i: (i, 0)
    fix = lambda i: (0, 0)
    in_specs = [pl.BlockSpec((tm, k), row) for _ in acts]
    in_specs += [pl.BlockSpec((k, d), fix), pl.BlockSpec((tm, d), row),
                 pl.BlockSpec((1, d), fix), pl.BlockSpec((1, d), fix)]
    args = list(acts) + [w, x, g, b]
    out_shape = [jax.ShapeDtypeStruct((nt, d), F32)]
    out_specs = [pl.BlockSpec((tm, d), row)]
    if router is not None:
        in_specs += [pl.BlockSpec((d, LANE), fix), pl.BlockSpec((1, LANE), fix)]
        args += list(router)
        out_shape += [jax.ShapeDtypeStruct((nt, LANE), jnp.int32),
                      jax.ShapeDtypeStruct((nt, LANE), F32)]
        out_specs += [pl.BlockSpec((tm, LANE), row), pl.BlockSpec((tm, LANE), row)]
    res = pl.pallas_call(
        functools.partial(_out_ln_kernel, n_act=len(acts), router=router is not None),
        grid=(nt // tm,),
        in_specs=in_specs, out_specs=out_specs, out_shape=out_shape,
        compiler_params=_cparams(("parallel",)),
        name="out_ln",
    )(*args)
    return res if router is not None else res[0]


def _ffn_kernel(x_ref, wg_ref, wu_ref, wd_ref, g_ref, b_ref, o_ref, xb_ref, acc_ref):
    c = pl.program_id(1)

    @pl.when(c == 0)
    def _():
        xb_ref[...] = x_ref[...].astype(BF16)
        acc_ref[...] = jnp.zeros_like(acc_ref)

    xb = xb_ref[...]
    gt = jnp.dot(xb, wg_ref[...], preferred_element_type=F32)
    up = jnp.dot(xb, wu_ref[...], preferred_element_type=F32)
    h = (gt * jax.nn.sigmoid(gt) * up).astype(BF16)
    acc_ref[...] += jnp.dot(h, wd_ref[...], preferred_element_type=F32)

    @pl.when(c == pl.num_programs(1) - 1)
    def _():
        o_ref[...] = _layer_norm(DN_ALPHA * x_ref[...] + acc_ref[...], g_ref[...], b_ref[...])


def ffn_ln(x, w_gu, w_down, g, b, tm=1024, tf=256):
    nt, d = x.shape
    dff = w_down.shape[0]
    nc = dff // tf
    return pl.pallas_call(
        _ffn_kernel,
        grid=(nt // tm, nc),
        in_specs=[pl.BlockSpec((tm, d), lambda i, c: (i, 0)),
                  pl.BlockSpec((d, tf), lambda i, c: (0, c)),
                  pl.BlockSpec((d, tf), lambda i, c: (0, nc + c)),
                  pl.BlockSpec((tf, d), lambda i, c: (c, 0)),
                  pl.BlockSpec((1, d), lambda i, c: (0, 0)),
                  pl.BlockSpec((1, d), lambda i, c: (0, 0))],
        out_specs=pl.BlockSpec((tm, d), lambda i, c: (i, 0)),
        out_shape=jax.ShapeDtypeStruct((nt, d), F32),
        scratch_shapes=[pltpu.VMEM((tm, d), BF16), pltpu.VMEM((tm, d), F32)],
        compiler_params=_cparams(("parallel", "arbitrary")),
        name="ffn_ln",
    )(x, w_gu, w_gu, w_down, g, b)


MOE_ROWS = 512
MOE_TF = 512
ROW_DMA_UNROLL = 8


def _dispatch_kernel(pos_ref, x_ref, xs_hbm, sem, *, tm):
    base = pl.program_id(0) * (2 * tm)

    def row_copy(k):
        return pltpu.make_async_copy(x_ref.at[pl.ds(k // 2, 1)],
                                     xs_hbm.at[pl.ds(pos_ref[base + k], 1)], sem)

    def start(k, carry):
        row_copy(k).start()
        return carry

    def wait(k, carry):
        row_copy(k).wait()
        return carry

    lax.fori_loop(0, 2 * tm, start, 0, unroll=ROW_DMA_UNROLL)
    lax.fori_loop(0, 2 * tm, wait, 0, unroll=ROW_DMA_UNROLL)


def moe_dispatch(x, pos, tm=512):
    nt, d = x.shape
    return pl.pallas_call(
        functools.partial(_dispatch_kernel, tm=tm),
        grid_spec=pltpu.PrefetchScalarGridSpec(
            num_scalar_prefetch=1, grid=(nt // tm,),
            in_specs=[pl.BlockSpec((tm, d), lambda i, p: (i, 0))],
            out_specs=pl.BlockSpec(memory_space=pl.ANY),
            scratch_shapes=[pltpu.SemaphoreType.DMA(())]),
        out_shape=jax.ShapeDtypeStruct((2 * nt, d), F32),
        compiler_params=pltpu.CompilerParams(dimension_semantics=("arbitrary",),
                                             vmem_limit_bytes=VMEM_LIMIT, has_side_effects=True),
        name="moe_dispatch",
    )(pos, x)


def _moe_kernel(tile_ref, exp_ref, lo_ref, hi_ref, first_ref,
                xs_ref, wg_ref, wu_ref, wd_ref, y_ref, xb_ref, acc_ref):
    w = pl.program_id(0)
    c = pl.program_id(1)
    lo = lo_ref[w]
    hi = hi_ref[w]

    @pl.when(hi > lo)
    def _():
        @pl.when(c == 0)
        def _():
            xb_ref[...] = xs_ref[...].astype(BF16)
            acc_ref[...] = jnp.zeros_like(acc_ref)

        xb = xb_ref[...]
        gt = jnp.dot(xb, wg_ref[...], preferred_element_type=F32)
        up = jnp.dot(xb, wu_ref[...], preferred_element_type=F32)
        h = (gt * jax.nn.sigmoid(gt) * up).astype(BF16)
        acc_ref[...] += jnp.dot(h, wd_ref[...], preferred_element_type=F32)

        @pl.when(c == pl.num_programs(1) - 1)
        def _():
            rows = lax.broadcasted_iota(jnp.int32, acc_ref.shape, 0)
            mine = (rows >= lo) & (rows < hi)

            @pl.when(first_ref[w] == 1)
            def _():
                y_ref[...] = jnp.where(mine, acc_ref[...], 0.0)

            @pl.when(first_ref[w] == 0)
            def _():
                y_ref[...] = jnp.where(mine, acc_ref[...], y_ref[...])


def moe_experts(xs, meta, w_gu, w_down, layer):
    r, d = xs.shape
    n_items = meta[0].shape[0]
    nc = D_FF_EXPERT // MOE_TF
    return pl.pallas_call(
        _moe_kernel,
        grid_spec=pltpu.PrefetchScalarGridSpec(
            num_scalar_prefetch=5, grid=(n_items, nc),
            in_specs=[
                pl.BlockSpec((MOE_ROWS, d), lambda w, c, t, e, lo, hi, f: (t[w], 0)),
                pl.BlockSpec((None, None, d, MOE_TF),
                             lambda w, c, t, e, lo, hi, f: (layer, e[w], 0, c)),
                pl.BlockSpec((None, None, d, MOE_TF),
                             lambda w, c, t, e, lo, hi, f: (layer, e[w], 0, nc + c)),
                pl.BlockSpec((None, None, MOE_TF, d),
                             lambda w, c, t, e, lo, hi, f: (layer, e[w], c, 0)),
            ],
            out_specs=pl.BlockSpec((MOE_ROWS, d), lambda w, c, t, e, lo, hi, f: (t[w], 0)),
            scratch_shapes=[pltpu.VMEM((MOE_ROWS, d), BF16), pltpu.VMEM((MOE_ROWS, d), F32)]),
        out_shape=jax.ShapeDtypeStruct((r, d), F32),
        compiler_params=_cparams(("arbitrary", "arbitrary")),
        name="moe_experts",
    )(*meta, xs, w_gu, w_gu, w_down)


def _combine_kernel(pos_ref, y_hbm, wt_ref, x_ref, g_ref, b_ref, o_ref, buf_ref, sem, *, tm):
    base = pl.program_id(0) * (2 * tm)

    def row_copy(k):
        return pltpu.make_async_copy(y_hbm.at[pl.ds(pos_ref[base + k], 1)],
                                     buf_ref.at[k % 2, pl.ds(k // 2, 1)], sem)

    def start(k, carry):
        row_copy(k).start()
        return carry

    def wait(k, carry):
        row_copy(k).wait()
        return carry

    lax.fori_loop(0, 2 * tm, start, 0, unroll=ROW_DMA_UNROLL)
    lax.fori_loop(0, 2 * tm, wait, 0, unroll=ROW_DMA_UNROLL)
    wt = wt_ref[...]
    y = wt[:, 0:1] * buf_ref[0] + wt[:, 1:2] * buf_ref[1]
    o_ref[...] = _layer_norm(DN_ALPHA * x_ref[...] + y, g_ref[...], b_ref[...])


def moe_combine_ln(y, pos, wt, x, g, b, tm=512):
    nt, d = x.shape
    return pl.pallas_call(
        functools.partial(_combine_kernel, tm=tm),
        grid_spec=pltpu.PrefetchScalarGridSpec(
            num_scalar_prefetch=1, grid=(nt // tm,),
            in_specs=[pl.BlockSpec(memory_space=pl.ANY),
                      pl.BlockSpec((tm, LANE), lambda i, p: (i, 0)),
                      pl.BlockSpec((tm, d), lambda i, p: (i, 0)),
                      pl.BlockSpec((1, d), lambda i, p: (0, 0)),
                      pl.BlockSpec((1, d), lambda i, p: (0, 0))],
            out_specs=pl.BlockSpec((tm, d), lambda i, p: (i, 0)),
            scratch_shapes=[pltpu.VMEM((2, tm, d), F32), pltpu.SemaphoreType.DMA(())]),
        out_shape=jax.ShapeDtypeStruct((nt, d), F32),
        compiler_params=_cparams(("arbitrary",)),
        name="moe_combine_ln",
    )(pos, y, wt, x, g, b)


def _moe_plan(idx):
    nt = idx.shape[0]
    r = 2 * nt
    e_flat = idx[:, :2].reshape(r)
    experts = jnp.arange(N_EXPERTS, dtype=jnp.int32)
    onehot = (e_flat[:, None] == experts[None, :]).astype(jnp.int32)
    csum = jnp.cumsum(onehot, axis=0)
    cnt = csum[-1]
    offs = jnp.cumsum(cnt) - cnt
    pos = jnp.sum(onehot * (csum - 1 + offs[None, :]), axis=1).astype(jnp.int32)
    n_tiles = r // MOE_ROWS
    n_items = n_tiles + N_EXPERTS - 1
    t_start = offs // MOE_ROWS
    t_end = (offs + cnt + MOE_ROWS - 1) // MOE_ROWS
    n_e = jnp.where(cnt > 0, t_end - t_start, 0)
    item_end = jnp.cumsum(n_e)
    item_start = item_end - n_e
    total = item_end[-1]
    w = jnp.arange(n_items, dtype=jnp.int32)
    e_w = jnp.minimum(jnp.sum((item_end[None, :] <= w[:, None]).astype(jnp.int32), axis=1),
                      N_EXPERTS - 1)
    tile_w = t_start[e_w] + (w - item_start[e_w])
    valid = w < total
    lo = jnp.maximum(offs[e_w], tile_w * MOE_ROWS) - tile_w * MOE_ROWS
    hi = jnp.minimum(offs[e_w] + cnt[e_w], (tile_w + 1) * MOE_ROWS) - tile_w * MOE_ROWS
    last = jnp.maximum(total - 1, 0)
    tile_w = jnp.where(valid, tile_w, tile_w[last]).astype(jnp.int32)
    e_w = jnp.where(valid, e_w, e_w[last]).astype(jnp.int32)
    lo = jnp.where(valid, lo, 0).astype(jnp.int32)
    hi = jnp.where(valid, hi, 0).astype(jnp.int32)
    prev = jnp.concatenate([jnp.full((1,), -1, jnp.int32), tile_w[:-1]])
    first = (valid & (tile_w != prev)).astype(jnp.int32)
    return pos, (tile_w, e_w, lo, hi, first)


def moe_ln(x, idx, wt, w_gu, w_down, layer, g, b):
    pos, meta = _moe_plan(idx)
    xs = moe_dispatch(x, pos)
    y = moe_experts(xs, meta, w_gu, w_down, layer)
    return moe_combine_ln(y, pos, wt, x, g, b)


def _gate_pair(g_ref, col0, lane):
    sel = lax.broadcasted_iota(jnp.int32, (LANE, LANE), 0) == (
        col0 + (lax.broadcasted_iota(jnp.int32, (LANE, LANE), 1) >= HEAD_DIM).astype(jnp.int32))
    return jnp.dot(g_ref[...], sel.astype(F32), preferred_element_type=F32, precision=HI)


def _softmax_step(carry, s, valid, v):
    m, l, acc = carry
    if valid is not None:
        s = jnp.where(valid, s, NEG)
    m_new = jnp.maximum(m, jnp.max(s, axis=1, keepdims=True))
    p = jnp.exp(s - m_new)
    if valid is not None:
        p = jnp.where(valid, p, 0.0)
    alpha = jnp.exp(m - m_new)
    l = alpha * l + jnp.sum(p, axis=1, keepdims=True)
    acc = alpha * acc + jnp.dot(p.astype(BF16), v, preferred_element_type=F32)
    return m_new, l, acc


def _pair_attn_kernel(*refs, mode, t):
    if mode == "fox":
        q_ref, k_ref, v_ref, c_ref, o_ref = refs
    elif mode == "sel":
        q_ref, k_ref, v_ref, sm_ref, e_ref, g_ref, o_ref = refs
    else:
        q_ref, k_ref, v_ref, g_ref, o_ref = refs
    j = pl.program_id(1)
    qi = pl.program_id(2)
    q = q_ref[...]
    lane = lax.broadcasted_iota(jnp.int32, (t, LANE), 1)
    row = lax.broadcasted_iota(jnp.int32, (t, t), 0)
    col = lax.broadcasted_iota(jnp.int32, (t, t), 1)
    causal = col <= row
    sm = sm_ref[...] if mode == "sel" else None
    halves = []
    for half in range(2):
        qh = jnp.where((lane < HEAD_DIM) == (half == 0), q, jnp.zeros_like(q))
        if mode == "fox":
            cq_row = c_ref[half, pl.ds(qi, 1), :]
            cq = jnp.sum(jnp.where(row == col, cq_row, 0.0), axis=1, keepdims=True)

        def step(kb, carry, kind):
            k0 = pl.multiple_of(kb * t, t)
            k = k_ref[pl.ds(k0, t), :]
            v = v_ref[pl.ds(k0, t), :]
            s = _dot_nt(qh, k)
            valid = None
            if mode == "fox":
                s = s + (cq - c_ref[half, pl.ds(kb, 1), :])
            if mode == "sel":
                valid = jnp.dot(sm, e_ref[kb], preferred_element_type=F32) > 0.5
            if kind == "diag":
                valid = causal if valid is None else (valid & causal)
            if kind == "band":
                valid = (col >= row) & (qi > 0)
            return _softmax_step(carry, s, valid, v)

        carry = (jnp.full((t, 1), NEG, F32), jnp.zeros((t, 1), F32), jnp.zeros((t, LANE), F32))
        if mode == "win":
            carry = step(jnp.maximum(qi - 1, 0), carry, "band")
        else:
            carry = lax.fori_loop(0, qi, lambda kb, c: step(kb, c, "full"), carry)
        m, l, acc = step(qi, carry, "diag")
        halves.append(acc / jnp.maximum(l, 1e-30))
    o = jnp.where(lane < HEAD_DIM, halves[0], halves[1])
    if mode != "fox":
        branch = 1 if mode == "sel" else 2
        o = o * _gate_pair(g_ref, branch * N_HEADS + 2 * j, lane)
    o_ref[...] = o.astype(o_ref.dtype)


def pair_attention(mode, q, k, v, kcol, vcol, bsz, seq, extra):
    t = ATT_T
    nq = seq // t
    n_pairs = N_HEADS // 2
    in_specs = [pl.BlockSpec((t, LANE), lambda b, j, i: (b * nq + i, j)),
                pl.BlockSpec((seq, LANE), lambda b, j, i: (b, kcol(j))),
                pl.BlockSpec((seq, LANE), lambda b, j, i: (b, vcol(j)))]
    if mode == "fox":
        in_specs.append(pl.BlockSpec((None, None, 2, nq, t), lambda b, j, i: (b, j, 0, 0, 0)))
    elif mode == "sel":
        in_specs += [pl.BlockSpec((t, LANE), lambda b, j, i: (b * nq + i, 0)),
                     pl.BlockSpec((None, nq, LANE, t), lambda b, j, i: (j // 2, 0, 0, 0)),
                     pl.BlockSpec((t, LANE), lambda b, j, i: (b * nq + i, 0))]
    else:
        in_specs.append(pl.BlockSpec((t, LANE), lambda b, j, i: (b * nq + i, 0)))
    return pl.pallas_call(
        functools.partial(_pair_attn_kernel, mode=mode, t=t),
        grid=(bsz, n_pairs, nq),
        in_specs=in_specs,
        out_specs=pl.BlockSpec((t, LANE), lambda b, j, i: (b * nq + i, j)),
        out_shape=jax.ShapeDtypeStruct((bsz * seq, D_MODEL), BF16),
        compiler_params=_cparams(("parallel", "parallel", "arbitrary")),
        name="attn_" + mode,
    )(q, k, v, *extra)


def _gelu(x):
    return 0.5 * x * (1.0 + jnp.tanh(0.7978845608028654 * (x + 0.044715 * (x * x * x))))


def _compress_body(xs, w1_ref, b1_ref, pe_ref, w2d_ref, w2s_ref, dup_ref, nat_ref):
    n_chunk = xs[0].shape[0] // CMP_STRIDE
    n_rp = CMP_STRIDE // 2
    for which in range(2):
        nat = jnp.zeros((n_chunk, KV_W), F32)
        for cb in range(2):
            x_ref = xs[2 * which + cb]
            top = jnp.zeros((n_chunk, 2 * CMP_HIDDEN), F32)
            bot = jnp.zeros((n_chunk, 2 * CMP_HIDDEN), F32)
            for rp in range(n_rp):
                lhs = jnp.concatenate(
                    [x_ref[pl.ds(2 * rp, n_chunk, stride=CMP_STRIDE), :],
                     x_ref[pl.ds(2 * rp + 1, n_chunk, stride=CMP_STRIDE), :]], axis=1)
                i_top = which * 2 * n_rp + rp
                i_bot = i_top + n_rp
                top = top + jnp.dot((lhs + pe_ref[i_top]).astype(BF16), w1_ref[i_top],
                                    preferred_element_type=F32)
                bot = bot + jnp.dot((lhs + pe_ref[i_bot]).astype(BF16), w1_ref[i_bot],
                                    preferred_element_type=F32)
            hid = _gelu(top + pltpu.roll(bot, n_chunk - 1, 0) + b1_ref[which]).astype(BF16)
            for gg in range(2):
                g = 2 * cb + gg
                h = hid[:, gg * CMP_HIDDEN:(gg + 1) * CMP_HIDDEN]
                dup_ref[which, g] = jnp.dot(h, w2d_ref[which],
                                            preferred_element_type=F32).astype(dup_ref.dtype)
                nat = nat + jnp.dot(h, w2s_ref[which, g], preferred_element_type=F32)
        nat_ref[which] = nat.astype(nat_ref.dtype)


def _compress_kernel(x0_ref, x1_ref, x2_ref, x3_ref, *rest):
    _compress_body((x0_ref, x1_ref, x2_ref, x3_ref), *rest)


def _compress_pages_kernel(pt_ref, *refs, n_pages):
    pages = refs[:n_pages]
    rest = refs[n_pages:n_pages + 7]
    cols = refs[n_pages + 7:]
    for p in range(n_pages):
        x = pages[p][...].T
        for c in range(4):
            cols[c][p * PAGE:(p + 1) * PAGE, :] = x[:, c * LANE:(c + 1) * LANE]
    _compress_body(cols, *rest)


def _compress_out(n_seq, n_chunk):
    specs = [pl.BlockSpec((None, 2, NSA_G, n_chunk, LANE), lambda s, *_: (s, 0, 0, 0, 0)),
             pl.BlockSpec((None, 2, n_chunk, KV_W), lambda s, *_: (s, 0, 0, 0))]
    shapes = [jax.ShapeDtypeStruct((n_seq, 2, NSA_G, n_chunk, LANE), BF16),
              jax.ShapeDtypeStruct((n_seq, 2, n_chunk, KV_W), BF16)]
    return specs, shapes


def compress_pages(pt, cache_t, n_seq, n_pages, w1, b1, pe, w2d, w2s):
    seq = n_pages * PAGE
    fix3 = lambda s, pt: (0, 0, 0)

    def page_spec(p):
        return pl.BlockSpec((None, 2 * KV_W, PAGE), lambda s, pt: (pt[s * n_pages + p], 0, 0))

    out_specs, out_shape = _compress_out(n_seq, seq // CMP_STRIDE)
    return pl.pallas_call(
        functools.partial(_compress_pages_kernel, n_pages=n_pages),
        grid_spec=pltpu.PrefetchScalarGridSpec(
            num_scalar_prefetch=1, grid=(n_seq,),
            in_specs=[page_spec(p) for p in range(n_pages)] + [
                pl.BlockSpec(w1.shape, fix3), pl.BlockSpec(b1.shape, fix3),
                pl.BlockSpec(pe.shape, fix3), pl.BlockSpec(w2d.shape, fix3),
                pl.BlockSpec(w2s.shape, lambda s, pt: (0, 0, 0, 0))],
            out_specs=out_specs,
            scratch_shapes=[pltpu.VMEM((seq, LANE), F32) for _ in range(4)]),
        out_shape=out_shape,
        compiler_params=_cparams(("parallel",)),
        name="nsa_compress_pages",
    )(pt, *([cache_t] * n_pages), w1, b1, pe, w2d, w2s)


def compress(x, n_seq, seq, w1, b1, pe, w2d, w2s):
    fix = lambda s: (0, 0, 0)
    col = lambda c: pl.BlockSpec((seq, LANE), lambda s: (s, c))
    out_specs, out_shape = _compress_out(n_seq, seq // CMP_STRIDE)
    return pl.pallas_call(
        _compress_kernel,
        grid=(n_seq,),
        in_specs=[col(0), col(1), col(2), col(3),
                  pl.BlockSpec(w1.shape, fix), pl.BlockSpec(b1.shape, fix),
                  pl.BlockSpec(pe.shape, fix), pl.BlockSpec(w2d.shape, fix),
                  pl.BlockSpec(w2s.shape, lambda s: (0, 0, 0, 0))],
        out_specs=out_specs,
        out_shape=out_shape,
        compiler_params=_cparams(("parallel",)),
        name="nsa_compress",
    )(x, x, x, x, w1, b1, pe, w2d, w2s)


def _block_rank(score, blk, group):
    rank = jnp.zeros(score.shape, jnp.int32)
    for k in range(1, group):
        if group == LANE:
            other = pltpu.roll(score, k, 1)
            io = jnp.where(blk >= k, blk - k, blk - k + group)
        else:
            wrap = blk >= k
            other = jnp.where(wrap, pltpu.roll(score, k, 1), pltpu.roll(score, k + LANE - group, 1))
            io = jnp.where(wrap, blk - k, blk - k + group)
        beats = (other > score) | ((other == score) & (io < blk))
        rank = rank + beats.astype(jnp.int32)
    return rank


def _cmp_prompt_kernel(q_ref, kc_ref, vc_ref, g_ref, ov_ref, oc_ref, sm_ref, *, t):
    qi = pl.program_id(1)
    lane = lax.broadcasted_iota(jnp.int32, (t, LANE), 1)
    pos = qi * t + lax.broadcasted_iota(jnp.int32, (t, LANE), 0)
    n_cmp = kc_ref.shape[1] - 1
    valid = (lane * CMP_STRIDE + CMP_BLOCK - 1 <= pos) & (lane < n_cmp)
    gates = g_ref[...]
    imp = jnp.zeros((t, LANE), F32)
    for g in range(NSA_G):
        kc = kc_ref[g]
        vc = vc_ref[g]
        psum = jnp.zeros((t, LANE), F32)
        for pr in range(2):
            pair = 2 * g + pr
            q = q_ref[:, pair * LANE:(pair + 1) * LANE]
            outs = []
            for half in range(2):
                qh = jnp.where((lane < HEAD_DIM) == (half == 0), q, jnp.zeros_like(q))
                s = jnp.where(valid, _dot_nt(qh, kc), NEG)
                m = jnp.max(s, axis=1, keepdims=True)
                p = jnp.where(valid, jnp.exp(s - m), 0.0)
                p = p / jnp.maximum(jnp.sum(p, axis=1, keepdims=True), 1e-30)
                psum = psum + p
                outs.append(jnp.dot(p.astype(BF16), vc, preferred_element_type=F32))
            h0 = 2 * pair
            gate = jnp.where(lane < HEAD_DIM, gates[:, h0:h0 + 1], gates[:, h0 + 1:h0 + 2])
            o = jnp.where(lane < HEAD_DIM, outs[0], outs[1]) * gate
            oc_ref[:, pair * LANE:(pair + 1) * LANE] = o.astype(oc_ref.dtype)
        imp = imp + jnp.dot(psum, ov_ref[g], preferred_element_type=F32, precision=HI)
    n_blk = LANE // NSA_G
    blk = lane % n_blk
    cur = pos // SEL_BLOCK
    forced = (blk == 0) | (cur - blk < N_LOCAL_BLOCKS)
    score = jnp.where(blk <= cur, imp + jnp.where(forced, FORCE_BONUS, 0.0), NEG)
    rank = _block_rank(score, blk, n_blk)
    sm_ref[...] = jnp.where(rank < N_SELECT, 1.0, 0.0).astype(sm_ref.dtype)


def cmp_prompt(q, kc_dup, vc_dup, gates, ov, bsz, seq):
    t = ATT_T
    nq = seq // t
    n_chunk = kc_dup.shape[2]
    return pl.pallas_call(
        functools.partial(_cmp_prompt_kernel, t=t),
        grid=(bsz, nq),
        in_specs=[pl.BlockSpec((t, D_MODEL), lambda b, i: (b * nq + i, 0)),
                  pl.BlockSpec((None, NSA_G, n_chunk, LANE), lambda b, i: (b, 0, 0, 0)),
                  pl.BlockSpec((None, NSA_G, n_chunk, LANE), lambda b, i: (b, 0, 0, 0)),
                  pl.BlockSpec((t, LANE), lambda b, i: (b * nq + i, 0)),
                  pl.BlockSpec((NSA_G, LANE, LANE), lambda b, i: (0, 0, 0))],
        out_specs=[pl.BlockSpec((t, D_MODEL), lambda b, i: (b * nq + i, 0)),
                   pl.BlockSpec((t, LANE), lambda b, i: (b * nq + i, 0))],
        out_shape=[jax.ShapeDtypeStruct((bsz * seq, D_MODEL), BF16),
                   jax.ShapeDtypeStruct((bsz * seq, LANE), BF16)],
        compiler_params=_cparams(("parallel", "arbitrary")),
        name="nsa_cmp_prompt",
    )(q, kc_dup, vc_dup, gates, ov)


def _softplus(x):
    return jnp.maximum(x, 0.0) + jnp.log1p(jnp.exp(-jnp.abs(x)))


def _rg_gates_and_scan(u3, prev3, gb, cw_ref, cb_ref, wg_ref, gab_ref, gxb_ref, lam_ref):
    nt8, _, c = u3.shape
    rows = nt8 * SUB
    rowi = lax.broadcasted_iota(jnp.int32, u3.shape, 1)

    def shifted(s):
        return jnp.where(rowi >= s, pltpu.roll(u3, s, 1), pltpu.roll(prev3, s, 1))

    cx3 = (cb_ref[...] + cw_ref[0:1, :] * shifted(3) + cw_ref[1:2, :] * shifted(2)
           + cw_ref[2:3, :] * shifted(1) + cw_ref[3:4, :] * u3)
    cx = cx3.reshape(rows, c)
    z = jnp.dot(cx.astype(BF16), wg_ref[...], preferred_element_type=F32)
    r = jax.nn.sigmoid(z[:, :c] + gab_ref[...])
    ig = jax.nn.sigmoid(z[:, c:] + gxb_ref[...])
    log_a = -RG_C * r * _softplus(-lam_ref[...])
    a = jnp.exp(log_a)
    b = jnp.sqrt(1.0 - jnp.exp(2.0 * log_a)) * (ig * cx)
    a3 = a.reshape(nt8, SUB, c)
    b3 = b.reshape(nt8, SUB, c)
    for s in (1, 2, 4):
        keep = rowi >= s
        b3 = jnp.where(keep, a3 * pltpu.roll(b3, s, 1) + b3, b3)
        a3 = jnp.where(keep, a3 * pltpu.roll(a3, s, 1), a3)
    return a3, b3, _gelu(gb)


def _rg_prompt_kernel(gb_ref, u_ref, cw_ref, cb_ref, wg_ref, gab_ref, gxb_ref, lam_ref,
                      y_ref, hl_ref, cu_ref, ch_ref, a_ref, b_ref, h_ref):
    rows, c = u_ref.shape
    nt8 = rows // SUB

    @pl.when(pl.program_id(1) == 0)
    def _():
        cu_ref[...] = jnp.zeros_like(cu_ref)
        ch_ref[...] = jnp.zeros_like(ch_ref)

    u3 = u_ref[...].reshape(nt8, SUB, c)
    prev3 = jnp.concatenate([cu_ref[...][None], u3[:nt8 - 1]], axis=0)
    a3, b3, gelu_gb = _rg_gates_and_scan(u3, prev3, gb_ref[...], cw_ref, cb_ref, wg_ref,
                                         gab_ref, gxb_ref, lam_ref)
    a_ref[...] = a3
    b_ref[...] = b3

    def body(j, hb):
        h = a_ref[j] * hb + b_ref[j]
        h_ref[j] = h
        return jnp.broadcast_to(h[SUB - 1:SUB, :], (SUB, c))

    hb = lax.fori_loop(0, nt8, body, ch_ref[...])
    ch_ref[...] = hb
    cu_ref[...] = u3[nt8 - 1]
    hl_ref[...] = hb
    y_ref[...] = (gelu_gb * h_ref[...].reshape(rows, c)).astype(y_ref.dtype)


def _rg_sample_kernel(gb_ref, u_ref, prev_ref, h0_ref, cw_ref, cb_ref, wg_ref, gab_ref, gxb_ref,
                      lam_ref, y_ref, h_ref):
    rows, c = u_ref.shape
    nt8 = rows // SUB
    u3 = u_ref[...].reshape(nt8, SUB, c)
    prev3 = prev_ref[...].reshape(nt8, SUB, c)
    a3, b3, gelu_gb = _rg_gates_and_scan(u3, prev3, gb_ref[...], cw_ref, cb_ref, wg_ref,
                                         gab_ref, gxb_ref, lam_ref)
    h = (a3 * h0_ref[...].reshape(nt8, SUB, c) + b3).reshape(rows, c)
    h_ref[...] = h
    y_ref[...] = (gelu_gb * h).astype(y_ref.dtype)


def _rg_weight_specs(n_grid):
    c = D_RNNP
    fix = (lambda b, i: (0, 0)) if n_grid == 2 else (lambda i: (0, 0))
    return [pl.BlockSpec((CONV_W, c), fix), pl.BlockSpec((1, c), fix),
            pl.BlockSpec((c, 2 * c), fix), pl.BlockSpec((1, c), fix),
            pl.BlockSpec((1, c), fix), pl.BlockSpec((1, c), fix)]


def rg_prompt(gb, u, weights, bsz, seq, rows=256):
    c = D_RNNP
    nc = seq // rows
    blk = lambda b, i: (b * nc + i, 0)
    return pl.pallas_call(
        _rg_prompt_kernel,
        grid=(bsz, nc),
        in_specs=[pl.BlockSpec((rows, c), blk), pl.BlockSpec((rows, c), blk)] + _rg_weight_specs(2),
        out_specs=[pl.BlockSpec((rows, c), blk), pl.BlockSpec((SUB, c), lambda b, i: (b, 0))],
        out_shape=[jax.ShapeDtypeStruct((bsz * seq, c), BF16),
                   jax.ShapeDtypeStruct((bsz * SUB, c), F32)],
        scratch_shapes=[pltpu.VMEM((SUB, c), F32), pltpu.VMEM((SUB, c), F32),
                        pltpu.VMEM((rows // SUB, SUB, c), F32),
                        pltpu.VMEM((rows // SUB, SUB, c), F32),
                        pltpu.VMEM((rows // SUB, SUB, c), F32)],
        compiler_params=_cparams(("parallel", "arbitrary")),
        name="rg_prompt",
    )(gb, u, *weights)


def rg_sample(gb, u, row0, prev, h0, weights, rows=256):
    c = D_RNNP
    n = prev.shape[0]
    while n % rows or row0 % rows:
        rows -= SUB
    blk = lambda i: (i, 0)
    off = lambda i: (row0 // rows + i, 0)
    return pl.pallas_call(
        _rg_sample_kernel,
        grid=(n // rows,),
        in_specs=[pl.BlockSpec((rows, c), off)] * 2 + [pl.BlockSpec((rows, c), blk)] * 2
        + _rg_weight_specs(1),
        out_specs=[pl.BlockSpec((rows, c), blk), pl.BlockSpec((rows, c), blk)],
        out_shape=[jax.ShapeDtypeStruct((n, c), BF16), jax.ShapeDtypeStruct((n, c), F32)],
        compiler_params=_cparams(("parallel",)),
        name="rg_sample",
    )(gb, u, prev, h0, *weights)


def _rows_of_heads(x16):
    n = x16.shape[1]
    return jnp.broadcast_to(x16[:, None, :], (N_HEADS, SUB, n)).reshape(N_HEADS * SUB, n)


def _pad_new_rows(x):
    return jnp.concatenate([x, jnp.zeros((PAGE - SUB, x.shape[1]), x.dtype)], axis=0)


def _logf_suffix_kernel(x_ref, exc_ref, tot_ref):
    n, h, _ = x_ref.shape
    x = x_ref[...].reshape(n * h, PAGE)
    r = lax.broadcasted_iota(jnp.int32, (PAGE, PAGE), 0)
    c = lax.broadcasted_iota(jnp.int32, (PAGE, PAGE), 1)
    exc = jnp.dot(x, (r > c).astype(F32), preferred_element_type=F32, precision=HI)
    tot = jnp.dot(x, jnp.ones((PAGE, PAGE), F32), preferred_element_type=F32, precision=HI)
    exc_ref[...] = exc.reshape(n, h, PAGE)
    tot_ref[...] = tot.reshape(n, h, PAGE)


def logf_suffix(logf_t):
    pool, h, _ = logf_t.shape
    n = 64 if pool % 64 == 0 else SUB
    spec = pl.BlockSpec((n, h, PAGE), lambda i: (i, 0, 0))
    return pl.pallas_call(
        _logf_suffix_kernel,
        grid=(pool // n,),
        in_specs=[spec], out_specs=[spec, spec],
        out_shape=[jax.ShapeDtypeStruct(logf_t.shape, F32)] * 2,
        compiler_params=_cparams(("parallel",)),
        name="logf_suffix",
    )(logf_t)


FOX_PAGES_PER_STEP = 4


def _fox_sample_kernel(pt_ref, q_ref, lfn_ref, kvn_ref, *refs):
    npp = FOX_PAGES_PER_STEP
    k_refs, v_refs = refs[:npp], refs[npp:2 * npp]
    exc_refs, tot_refs = refs[2 * npp:3 * npp], refs[3 * npp:4 * npp]
    o_ref, m_ref, l_ref, acc_ref, sfx_ref, cq_ref = refs[4 * npp:]
    step = pl.program_id(1)
    rows = N_HEADS * SUB
    q = q_ref[...]

    def update(s, valid, pv):
        if valid is not None:
            s = jnp.where(valid, s, NEG)
        m_prev = m_ref[...]
        m_new = jnp.maximum(m_prev, jnp.max(s, axis=1, keepdims=True))
        p = jnp.exp(s - m_new)
        if valid is not None:
            p = jnp.where(valid, p, 0.0)
        alpha = jnp.exp(m_prev - m_new)
        l_ref[...] = alpha * l_ref[...] + jnp.sum(p, axis=1, keepdims=True)
        acc_ref[...] = alpha * acc_ref[...] + pv(p.astype(BF16))
        m_ref[...] = m_new

    @pl.when(step == 0)
    def _():
        row = lax.broadcasted_iota(jnp.int32, (rows, LANE), 0)
        col = lax.broadcasted_iota(jnp.int32, (rows, LANE), 1)
        qpos = row % SUB
        m_ref[...] = jnp.full_like(m_ref, NEG)
        l_ref[...] = jnp.zeros_like(l_ref)
        acc_ref[...] = jnp.zeros_like(acc_ref)
        sfx_ref[...] = jnp.zeros_like(sfx_ref)
        lf = _pad_new_rows(lfn_ref[...])
        tri = (lax.broadcasted_iota(jnp.int32, (LANE, LANE), 0)
               >= lax.broadcasted_iota(jnp.int32, (LANE, LANE), 1)).astype(F32)
        cnew = jnp.dot(tri, lf, preferred_element_type=F32, precision=HI)
        head_of_row = (col == row // SUB).astype(F32)
        c_keys = _dot_nt(head_of_row, cnew, precision=HI)
        cq = jnp.sum(jnp.where(col == qpos, c_keys, 0.0), axis=1, keepdims=True)
        cq_ref[...] = cq
        k_new = _pad_new_rows(kvn_ref[:, 0:D_MODEL]).astype(BF16)
        v_new = _pad_new_rows(kvn_ref[:, D_MODEL:2 * D_MODEL]).astype(BF16)
        s = _dot_nt(q, k_new) + (cq - c_keys)
        update(s, col <= qpos, lambda p: jnp.dot(p, v_new, preferred_element_type=F32))

    @pl.when(step > 0)
    def _():
        run = sfx_ref[...]
        bias = [None] * npp
        for i in reversed(range(npp)):
            bias[i] = _rows_of_heads(exc_refs[i][...] + run)
            run = run + tot_refs[i][...]
        sfx_ref[...] = run
        s = jnp.concatenate(
            [jnp.dot(q, k_refs[i][...].astype(BF16), preferred_element_type=F32) + bias[i]
             for i in range(npp)], axis=1) + cq_ref[...]
        v_all = jnp.concatenate([v_refs[i][...].astype(BF16) for i in range(npp)], axis=1)
        update(s, None, lambda p: _dot_nt(p, v_all))

    @pl.when(step == pl.num_programs(1) - 1)
    def _():
        o = acc_ref[...] / jnp.maximum(l_ref[...], 1e-30)
        o3 = o.reshape(N_HEADS, SUB, D_MODEL)
        own = (lax.broadcasted_iota(jnp.int32, o3.shape, 2) // HEAD_DIM
               == lax.broadcasted_iota(jnp.int32, o3.shape, 0))
        o_ref[...] = jnp.sum(jnp.where(own, o3, 0.0), axis=0)


def fox_sample(pt, qbd, lf_new, kv, row0, cache_t, exc, tot, slot, n_seq, n_pages):
    rows = N_HEADS * SUB
    npp = FOX_PAGES_PER_STEP
    assert n_pages % npp == 0

    def page(i):
        return lambda b, s, pt: pt[b * n_pages + n_pages - jnp.maximum(s, 1) * npp + i]

    def cache_spec(i, half):
        pg = page(i)
        return pl.BlockSpec((None, None, D_MODEL, LANE), lambda b, s, pt: (slot, pg(b, s, pt), half, 0))

    def lf_spec(i):
        pg = page(i)
        return pl.BlockSpec((None, N_HEADS, LANE), lambda b, s, pt: (pg(b, s, pt), 0, 0))

    blk0 = row0 // SUB
    in_specs = [pl.BlockSpec((None, rows, D_MODEL), lambda b, s, pt: (b, 0, 0)),
                pl.BlockSpec((SUB, LANE), lambda b, s, pt: (blk0 + b, 0)),
                pl.BlockSpec((SUB, 2 * D_MODEL), lambda b, s, pt: (blk0 + b, 0))]
    in_specs += [cache_spec(i, 0) for i in range(npp)] + [cache_spec(i, 1) for i in range(npp)]
    in_specs += [lf_spec(i) for i in range(npp)] * 2
    return pl.pallas_call(
        _fox_sample_kernel,
        grid_spec=pltpu.PrefetchScalarGridSpec(
            num_scalar_prefetch=1, grid=(n_seq, n_pages // npp + 1),
            in_specs=in_specs,
            out_specs=pl.BlockSpec((SUB, D_MODEL), lambda b, s, pt: (b, 0)),
            scratch_shapes=[pltpu.VMEM((rows, 1), F32), pltpu.VMEM((rows, 1), F32),
                            pltpu.VMEM((rows, D_MODEL), F32), pltpu.VMEM((N_HEADS, LANE), F32),
                            pltpu.VMEM((rows, 1), F32)]),
        out_shape=jax.ShapeDtypeStruct((n_seq * SUB, D_MODEL), F32),
        compiler_params=_cparams(("parallel", "arbitrary")),
        name="fox_sample",
    )(pt, qbd, lf_new, kv, *([cache_t] * (2 * npp)), *([exc] * npp), *([tot] * npp))


def _nsa_sample_kernel(pt_ref, q_ref, g_ref, kv_ref, ov_ref, e_ref, win_ref, new_ref, *refs):
    pages, o_ref = refs[:-1], refs[-1]
    n_pages = len(pages)
    rows = N_HEADS * SUB
    q = q_ref[...]
    row = lax.broadcasted_iota(jnp.int32, (rows, LANE), 0)
    col = lax.broadcasted_iota(jnp.int32, (rows, LANE), 1)
    new_ok = col <= row % SUB
    new = _pad_new_rows(new_ref[:, 2 * KV_W:]).astype(BF16)

    def attend(scores, valids, values):
        scores = [jnp.where(v, s, NEG) for s, v in zip(scores, valids)]
        m = functools.reduce(jnp.maximum, [jnp.max(s, axis=1, keepdims=True) for s in scores])
        ps = [jnp.where(v, jnp.exp(s - m), 0.0) for s, v in zip(scores, valids)]
        den = functools.reduce(lambda a, b: a + b, [jnp.sum(p, axis=1, keepdims=True) for p in ps])
        out = functools.reduce(lambda a, b: a + b, [f(p.astype(BF16)) for f, p in zip(values, ps)])
        return out / jnp.maximum(den, 1e-30), ps, den

    n_cmp = kv_ref.shape[1] - 1
    vc = kv_ref[1]
    o_c, (p,), den = attend([_dot_nt(q, kv_ref[0])], [col < n_cmp],
                            [lambda pb: jnp.dot(pb, vc, preferred_element_type=F32)])
    p = p / jnp.maximum(den, 1e-30)
    psum = jnp.sum(p.reshape(NSA_G, NSA_HPG, SUB, LANE), axis=1).reshape(NSA_G * SUB, LANE)
    imp = jnp.dot(psum, ov_ref[...], preferred_element_type=F32, precision=HI)
    blk = lax.broadcasted_iota(jnp.int32, imp.shape, 1)
    cur = (n_pages * PAGE) // SEL_BLOCK
    forced = (blk == 0) | (cur - blk < N_LOCAL_BLOCKS)
    score = jnp.where(blk <= cur, imp + jnp.where(forced, FORCE_BONUS, 0.0), NEG)
    sel = (_block_rank(score, blk, LANE) < N_SELECT).astype(BF16)
    sel_rows = jnp.broadcast_to(sel.reshape(NSA_G, 1, SUB, LANE),
                                (NSA_G, NSA_HPG, SUB, LANE)).reshape(rows, LANE)
    key_ok = jnp.dot(sel_rows, e_ref[...], preferred_element_type=F32) > 0.5
    wrow = lax.broadcasted_iota(jnp.int32, (rows, WINDOW), 0) % SUB
    wcol = lax.broadcasted_iota(jnp.int32, (rows, WINDOW), 1)
    v_win = win_ref[KV_W:2 * KV_W, :].astype(BF16)
    o_w, _, _ = attend(
        [jnp.dot(q, win_ref[0:KV_W, :].astype(BF16), preferred_element_type=F32),
         _dot_nt(q, new[:, 2 * KV_W:3 * KV_W])],
        [wcol >= wrow, new_ok],
        [lambda pb: _dot_nt(pb, v_win),
         lambda pb: jnp.dot(pb, new[:, 3 * KV_W:4 * KV_W], preferred_element_type=F32)])
    k_all = jnp.concatenate([pg[0:KV_W, :].astype(BF16) for pg in pages], axis=1)
    v_all = jnp.concatenate([pg[KV_W:2 * KV_W, :].astype(BF16) for pg in pages], axis=1)
    past = n_pages * PAGE
    o_s, _, _ = attend(
        [jnp.dot(q, k_all, preferred_element_type=F32), _dot_nt(q, new[:, 0:KV_W])],
        [key_ok[:, :past], new_ok & key_ok[:, past:]],
        [lambda pb: _dot_nt(pb, v_all),
         lambda pb: jnp.dot(pb, new[:, KV_W:2 * KV_W], preferred_element_type=F32)])
    g = g_ref[...]
    o_ref[...] = g[:, 0:1] * o_c + g[:, 1:2] * o_s + g[:, 2:3] * o_w


def nsa_sample(pt, qbd, grow, kv_nat, ov, emat, win_t, kv, row0, cache_t, n_seq, n_pages):
    rows = N_HEADS * SUB
    n_chunk = kv_nat.shape[2]
    blk0 = row0 // SUB

    def page_spec(p):
        return pl.BlockSpec((None, 2 * KV_W, PAGE), lambda b, pt: (pt[b * n_pages + p], 1, 0))

    return pl.pallas_call(
        _nsa_sample_kernel,
        grid_spec=pltpu.PrefetchScalarGridSpec(
            num_scalar_prefetch=1, grid=(n_seq,),
            in_specs=[
                pl.BlockSpec((None, rows, KV_W), lambda b, pt: (b, 0, 0)),
                pl.BlockSpec((None, rows, LANE), lambda b, pt: (b, 0, 0)),
                pl.BlockSpec((None, 2, n_chunk, KV_W), lambda b, pt: (b, 0, 0, 0)),
                pl.BlockSpec((LANE, LANE), lambda b, pt: (0, 0)),
                pl.BlockSpec(emat.shape, lambda b, pt: (0, 0)),
                pl.BlockSpec((None, 2 * KV_W, WINDOW), lambda b, pt: (b, 0, 0)),
                pl.BlockSpec((SUB, NSA_KV), lambda b, pt: (blk0 + b, 0)),
            ] + [page_spec(p) for p in range(n_pages)],
            out_specs=pl.BlockSpec((None, rows, KV_W), lambda b, pt: (b, 0, 0))),
        out_shape=jax.ShapeDtypeStruct((n_seq, rows, KV_W), F32),
        compiler_params=_cparams(("parallel",)),
        name="nsa_sample",
    )(pt, qbd, grow, kv_nat, ov, emat, win_t, kv, *([cache_t] * n_pages))


def _pad_rows(a, n):
    return jnp.pad(a, ((0, n - a.shape[0]),) + ((0, 0),) * (a.ndim - 1))


def _pad_cols(a, n):
    return jnp.pad(a, ((0, 0),) * (a.ndim - 1) + ((0, n - a.shape[-1]),))


FOX_SEGS = ((0, D_MODEL, None, ((BF16, ATTN_SCALE),)),
            (D_MODEL, 2 * D_MODEL, None, ((F32, 1.0), (BF16, 1.0))),
            (3 * D_MODEL, LANE, "log_sigmoid", ((F32, 1.0),)))


def fox_mixer(x, dims, w_in, b_f, cache_kv, cache_logf, slot, pt):
    bp, tp, bs, ts, n_p, n_s = dims
    nt = n_p + n_s
    n_pages = pt.shape[0] // bs
    w = _pad_cols(w_in, 3 * D_MODEL + LANE).astype(BF16)
    bias = jnp.zeros((1, 3 * D_MODEL + LANE), F32).at[0, 3 * D_MODEL:3 * D_MODEL + N_HEADS].set(b_f)
    q, kv, kvb, logf = proj(x, w, bias, FOX_SEGS)
    nq = tp // ATT_T
    c = jnp.cumsum(logf[:n_p, :N_HEADS].reshape(bp, tp, N_HEADS), axis=1)
    c = jnp.transpose(c, (0, 2, 1)).reshape(bp, N_HEADS // 2, 2, nq, ATT_T)
    o_p = pair_attention("fox", q, kvb, kvb, lambda j: j, lambda j: N_HEADS // 2 + j, bp, tp, (c,))
    head = jnp.arange(D_MODEL, dtype=jnp.int32) // HEAD_DIM
    q_s = q[n_p:nt].reshape(bs, 1, ts, D_MODEL)
    own = (head[None, :] == jnp.arange(N_HEADS, dtype=jnp.int32)[:, None])[None, :, None, :]
    qbd = jnp.where(own, q_s, jnp.zeros_like(q_s)).reshape(bs, N_HEADS * ts, D_MODEL)
    cache_t = jnp.transpose(cache_kv, (0, 1, 3, 4, 5, 2)).reshape(
        cache_kv.shape[0], cache_kv.shape[1], 2 * D_MODEL, PAGE)
    exc, tot = logf_suffix(jnp.transpose(cache_logf[slot], (0, 2, 1)))
    o_s = fox_sample(pt, qbd, logf, kv, n_p, cache_t, exc, tot, slot, bs, n_pages)
    act = _pad_rows(jnp.concatenate([o_p, o_s.astype(BF16)], 0), x.shape[0])
    kv_p = kv[:n_p].reshape(bp, tp, 2, N_HEADS, HEAD_DIM)
    kv_s = kv[n_p:nt].reshape(bs, ts, 2, N_HEADS, HEAD_DIM)
    lf_p = logf[:n_p, :N_HEADS].reshape(bp, tp, N_HEADS)
    lf_s = logf[n_p:nt, :N_HEADS].reshape(bs, ts, N_HEADS)
    return act, (kv_p, kv_s, lf_p, lf_s)


def _rg_pad(a):
    shp = a.shape[:-1]
    a = a.reshape(shp + (RG_BLOCKS, RG_BS))
    return _pad_cols(a, RG_BSP).reshape(shp + (D_RNNP,))


RG_SEGS = ((0, D_RNNP, None, ((F32, 1.0),)), (D_RNNP, D_RNNP, None, ((F32, 1.0),)))


def rg_mixer(x, dims, w_in, conv_w, conv_b, ga_w, ga_b, gx_w, gx_b, lam, h0, conv_buf):
    bp, tp, bs, ts, n_p, n_s = dims
    nt = n_p + n_s
    w = jnp.concatenate([_rg_pad(w_in[:, :D_RNN]), _rg_pad(w_in[:, D_RNN:])], axis=1).astype(BF16)
    gb, u = proj(x, w, jnp.zeros((1, 2 * D_RNNP), F32), RG_SEGS)
    eye = jnp.eye(RG_BLOCKS, dtype=F32)

    def block_diag(wb):
        wp = jnp.pad(wb, ((0, 0), (0, RG_BSP - RG_BS), (0, RG_BSP - RG_BS)))
        return jnp.einsum("nij,nm->nimj", wp, eye).reshape(D_RNNP, D_RNNP)

    wg = jnp.concatenate([block_diag(ga_w), block_diag(gx_w)], axis=1).astype(BF16)
    weights = (_rg_pad(conv_w), _rg_pad(conv_b)[None], wg, _rg_pad(ga_b)[None],
               _rg_pad(gx_b)[None], _rg_pad(lam)[None])
    y_p, h_last = rg_prompt(gb, u, weights, bp, tp)
    prev = jnp.pad(_rg_pad(conv_buf), ((0, 0), (SUB - (CONV_W - 1), 0), (0, 0))).reshape(n_s, D_RNNP)
    h0r = jnp.broadcast_to(_rg_pad(h0)[:, None, :], (bs, ts, D_RNNP)).reshape(n_s, D_RNNP)
    y_s, h_s = rg_sample(gb, u, n_p, prev, h0r, weights)
    act = _pad_rows(jnp.concatenate([y_p, y_s], 0), x.shape[0])

    def unpad(a):
        return a.reshape(a.shape[:-1] + (RG_BLOCKS, RG_BSP))[..., :RG_BS].reshape(a.shape[:-1] + (D_RNN,))

    u_p = u[:n_p].reshape(bp, tp, D_RNNP)
    u_s = u[n_p:nt].reshape(bs, ts, D_RNNP)
    states = (unpad(h_last.reshape(bp, SUB, D_RNNP)[:, SUB - 1]),
              unpad(h_s.reshape(bs, ts, D_RNNP)[:, ts - 1]),
              unpad(u_p[:, tp - (CONV_W - 1):]), unpad(u_s[:, ts - (CONV_W - 1):]))
    return act, states


NSA_KV = 6 * KV_W
NSA_SEGS = ((0, D_MODEL, None, ((BF16, ATTN_SCALE),)),
            (D_MODEL, NSA_KV, None, ((F32, 1.0), (BF16, 1.0))),
            (D_MODEL + NSA_KV, LANE, "sigmoid", ((F32, 1.0),)))


def _overlap(n_sel_lanes, lane0):
    c0 = jnp.arange(LANE, dtype=jnp.int32)[:, None] * CMP_STRIDE
    lane = jnp.arange(LANE, dtype=jnp.int32)[None, :]
    s0 = (lane - lane0) * SEL_BLOCK
    hit = (c0 < s0 + SEL_BLOCK) & (c0 + CMP_BLOCK > s0) & (lane >= lane0) & (lane < lane0 + n_sel_lanes)
    return hit.astype(F32)


def _dup_groups(a):
    a = a.reshape(a.shape[0], NSA_G, 1, HEAD_DIM)
    return jnp.broadcast_to(a, (a.shape[0], NSA_G, 2, HEAD_DIM)).reshape(a.shape[0], 2 * KV_W)


def nsa_mixer(x, dims, w_in, b_gate, cmp_pe, cmp_w1, cmp_b1, cmp_w2, cache_kv, win_buf, pt):
    bp, tp, bs, ts, n_p, n_s = dims
    nt = n_p + n_s
    n_pages = pt.shape[0] // bs
    past = n_pages * PAGE
    n_w = D_MODEL + NSA_KV + LANE
    w = _pad_cols(w_in, n_w).astype(BF16)
    n_gate = 3 * N_HEADS
    bias = jnp.zeros((1, n_w), F32).at[0, D_MODEL + NSA_KV:D_MODEL + NSA_KV + n_gate].set(b_gate)
    q, kv, kvb, gates = proj(x, w, bias, NSA_SEGS)
    n_rp = CMP_STRIDE // 2
    eye2 = jnp.eye(2, dtype=F32)
    w1 = cmp_w1.reshape(2, 2, n_rp, 2, HEAD_DIM, CMP_HIDDEN)
    w1 = jnp.einsum("whpidn,gm->whpigdmn", w1, eye2)
    w1 = w1.reshape(2 * 2 * n_rp, 4 * HEAD_DIM, 2 * CMP_HIDDEN).astype(BF16)
    b1 = jnp.concatenate([cmp_b1, cmp_b1], axis=-1)[:, None, :]
    pe = jnp.broadcast_to(cmp_pe.reshape(2, 2, n_rp, 2, 1, HEAD_DIM), (2, 2, n_rp, 2, 2, HEAD_DIM))
    pe = pe.reshape(2 * 2 * n_rp, 1, 4 * HEAD_DIM)
    w2d = jnp.concatenate([cmp_w2, cmp_w2], axis=-1).astype(BF16)
    slab = jnp.eye(NSA_G, dtype=F32)
    w2s = jnp.einsum("wkd,gm->wgkmd", cmp_w2, slab).reshape(2, NSA_G, CMP_HIDDEN, KV_W).astype(BF16)
    kc_dup, _ = compress(kv, bp, tp, w1, b1, pe, w2d, w2s)
    ov_p = jnp.stack([_overlap(LANE // NSA_G, g * (LANE // NSA_G)) for g in range(NSA_G)])
    o_c, selmask = cmp_prompt(q, kc_dup[:, 0], kc_dup[:, 1], gates, ov_p, bp, tp)
    nq = tp // ATT_T
    key_blk = jnp.arange(tp, dtype=jnp.int32) // SEL_BLOCK
    lane = jnp.arange(LANE, dtype=jnp.int32)
    emat = (lane[None, :, None] == (jnp.arange(NSA_G, dtype=jnp.int32)[:, None, None] * (LANE // NSA_G)
                                    + key_blk[None, None, :])).astype(BF16)
    emat = jnp.transpose(emat.reshape(NSA_G, LANE, nq, ATT_T), (0, 2, 1, 3))
    kvb_p = kvb[:n_p]
    k_sel, v_sel, k_win, v_win = (_dup_groups(kvb_p[:, i * KV_W:(i + 1) * KV_W]) for i in range(2, 6))
    grp = lambda j: j // 2
    o_s = pair_attention("sel", q, k_sel, v_sel, grp, grp, bp, tp, (selmask, emat, gates))
    o_w = pair_attention("win", q, k_win, v_win, grp, grp, bp, tp, (gates,))
    cache_t = jnp.transpose(cache_kv, (0, 2, 3, 4, 1)).reshape(cache_kv.shape[0], 4 * KV_W, PAGE)
    _, kv_nat = compress_pages(pt, cache_t, bs, n_pages, w1, b1, pe, w2d, w2s)
    q_s = q[n_p:nt].reshape(bs, ts, N_HEADS, 1, HEAD_DIM)
    grp_of_head = jnp.arange(N_HEADS, dtype=jnp.int32) // NSA_HPG
    own = (grp_of_head[:, None] == jnp.arange(NSA_G, dtype=jnp.int32)[None, :])[None, None, :, :, None]
    qbd = jnp.where(own, q_s, jnp.zeros_like(q_s))
    qbd = jnp.transpose(qbd, (0, 2, 1, 3, 4)).reshape(bs, N_HEADS * ts, KV_W)
    g_s = gates[n_p:nt, :n_gate].reshape(bs, ts, 3, N_HEADS)
    grow = _pad_cols(jnp.transpose(g_s, (0, 3, 1, 2)).reshape(bs, N_HEADS * ts, 3), LANE)
    n_sel = -(-(past + ts) // SEL_BLOCK)
    ov_s = _overlap(n_sel, 0)
    keys = jnp.arange((n_pages + 1) * PAGE, dtype=jnp.int32) // SEL_BLOCK
    emat_s = (lane[:, None] == keys[None, :]).astype(BF16)
    win_t = jnp.transpose(win_buf, (0, 2, 3, 4, 1)).reshape(bs, 2 * KV_W, win_buf.shape[1])
    kv_s = kv[n_p:nt].reshape(bs, ts, NSA_KV)
    y_rows = nsa_sample(pt, qbd, grow, kv_nat, ov_s, emat_s, win_t, kv, n_p, cache_t, bs, n_pages)
    y5 = y_rows.reshape(bs, NSA_G, NSA_HPG, ts, NSA_G, HEAD_DIM)
    y_s = jnp.stack([y5[:, g, :, :, g, :] for g in range(NSA_G)], axis=1)
    y_s = jnp.transpose(y_s, (0, 3, 1, 2, 4)).reshape(n_s, D_MODEL).astype(BF16)
    pad = x.shape[0] - n_p
    acts = [_pad_rows(jnp.concatenate([o_c, y_s], 0), x.shape[0]),
            _pad_rows(o_s, x.shape[0]), _pad_rows(o_w, x.shape[0])]
    kv_p6 = kv[:n_p].reshape(bp, tp, 6, NSA_G, HEAD_DIM)
    kv_s6 = kv_s.reshape(bs, ts, 6, NSA_G, HEAD_DIM)
    w_keep = min(WINDOW, tp)
    win_new = jnp.concatenate([win_buf, kv_s6[:, :, 4:]], axis=1)[:, ts:]
    return acts, (kv_p6[:, :, :4], kv_s6[:, :, :4], kv_p6[:, tp - w_keep:, 4:], win_new)


def kernel(x_prompt, x_sample, cache_fox_kv, cache_fox_logf, state_rg_h, state_rg_conv, cache_nsa_kv, state_nsa_win, page_table, fox_w_in, fox_b_f, fox_w_out, rg_w_in, rg_conv_w, rg_conv_b, rg_gate_a_w, rg_gate_a_b, rg_gate_x_w, rg_gate_x_b, rg_lambda, rg_w_out, nsa_w_in, nsa_b_gate, nsa_cmp_pe, nsa_cmp_w1, nsa_cmp_b1, nsa_cmp_w2, nsa_w_out, ffn_w_gu, ffn_w_down, moe_router_w, moe_router_b, moe_w_gu, moe_w_down, ln_mix_g, ln_mix_b, ln_ffn_g, ln_ffn_b):
    bp, tp, d = x_prompt.shape
    bs, ts, _ = x_sample.shape
    assert ts == SUB and tp % ATT_T == 0 and WINDOW == ATT_T
    n_p, n_s = bp * tp, bs * ts
    nt = n_p + n_s
    dims = (bp, tp, bs, ts, n_p, n_s)
    ntp = -(-nt // TOKEN_TILE) * TOKEN_TILE
    x = _pad_rows(jnp.concatenate([x_prompt.reshape(n_p, d), x_sample.reshape(n_s, d)], 0), ntp)
    pt = page_table.reshape(-1)
    moe_gu, moe_down = moe_w_gu.astype(BF16), moe_w_down.astype(BF16)
    fox_out, rg_out, nsa_out = [], [], []
    for i in range(DEPTH):
        kind, slot = i % 3, i // 3
        if kind == 0:
            act, new = fox_mixer(x, dims, fox_w_in[slot], fox_b_f[slot], cache_fox_kv,
                                 cache_fox_logf, slot, pt)
            fox_out.append(new)
            acts, w_out = [act], fox_w_out[slot]
        elif kind == 1:
            act, new = rg_mixer(x, dims, rg_w_in[slot], rg_conv_w[slot], rg_conv_b[slot],
                                rg_gate_a_w[slot], rg_gate_a_b[slot], rg_gate_x_w[slot],
                                rg_gate_x_b[slot], rg_lambda[slot], state_rg_h[slot],
                                state_rg_conv[slot])
            rg_out.append(new)
            acts = [act]
            w_out = jnp.pad(rg_w_out[slot].reshape(RG_BLOCKS, RG_BS, d),
                            ((0, 0), (0, RG_BSP - RG_BS), (0, 0))).reshape(D_RNNP, d)
        else:
            acts, new = nsa_mixer(x, dims, nsa_w_in[slot], nsa_b_gate[slot], nsa_cmp_pe[slot],
                                  nsa_cmp_w1[slot], nsa_cmp_b1[slot], nsa_cmp_w2[slot],
                                  cache_nsa_kv[slot], state_nsa_win[slot], pt)
            nsa_out.append(new)
            w_out = nsa_w_out[slot]
        w_o = w_out.astype(BF16)
        j = i // 2
        g_m, b_m = ln_mix_g[i][None], ln_mix_b[i][None]
        g_f, b_f = ln_ffn_g[i][None], ln_ffn_b[i][None]
        if i % 2 == 0:
            x = out_ln(acts, w_o, x, g_m, b_m)
            x = ffn_ln(x, ffn_w_gu[j].astype(BF16), ffn_w_down[j].astype(BF16), g_f, b_f)
        else:
            rw_p = _pad_cols(moe_router_w[j], LANE)
            rb_p = _pad_cols(moe_router_b[j][None], LANE)
            x, idx, wt = out_ln(acts, w_o, x, g_m, b_m, router=(rw_p, rb_p))
            x = moe_ln(x, idx, wt, moe_gu, moe_down, j, g_f, b_f)
    xp = x[:n_p].reshape(bp, tp, d)
    xs = x[n_p:nt].reshape(bs, ts, d)
    stack = lambda outs, k: jnp.stack([o[k] for o in outs])
    return (xp, xs,
            stack(fox_out, 0), stack(fox_out, 1), stack(fox_out, 2), stack(fox_out, 3),
            stack(rg_out, 0), stack(rg_out, 1), stack(rg_out, 2), stack(rg_out, 3),
            stack(nsa_out, 0), stack(nsa_out, 1), stack(nsa_out, 2), stack(nsa_out, 3))
```

```python
import functools

import jax
import jax.numpy as jnp
from jax import lax
from jax.experimental import pallas as pl
from jax.experimental.pallas import tpu as pltpu

F32 = jnp.float32
BF16 = jnp.bfloat16
HI = lax.Precision.HIGHEST

D_MODEL = 1024
DEPTH = 4
N_HEADS = 16
HEAD_DIM = 64
ATTN_SCALE = HEAD_DIM ** -0.5
D_RNN = 1344
RG_BLOCKS = 16
RG_BS = 84
RG_BSP = 88
D_RNNP = RG_BLOCKS * RG_BSP
CONV_W = 4
RG_C = 8.0
NSA_G = 4
NSA_HPG = 4
KV_W = NSA_G * HEAD_DIM
CMP_BLOCK = 32
CMP_STRIDE = 16
CMP_HIDDEN = 256
SEL_BLOCK = 64
N_SELECT = 16
N_LOCAL_BLOCKS = 2
FORCE_BONUS = 1.0e4
WINDOW = 512
PAGE = 128
N_EXPERTS = 8
D_FF_EXPERT = 3584
DN_ALPHA = (2 * DEPTH) ** 0.25
LN_EPS = 1e-5
NEG = -1.0e30
LANE = 128
SUB = 8
TOKEN_TILE = 1024
ATT_T = 512
VMEM_LIMIT = 56 * 1024 * 1024


def _cparams(sem):
    return pltpu.CompilerParams(dimension_semantics=sem, vmem_limit_bytes=VMEM_LIMIT)


def _dot_nt(a, b, **kw):
    return lax.dot_general(a, b, (((1,), (1,)), ((), ())), preferred_element_type=F32, **kw)


def _log_sigmoid(x):
    return jnp.minimum(x, 0.0) - jnp.log1p(jnp.exp(-jnp.abs(x)))


def _proj_kernel(x_ref, w_ref, b_ref, *out_refs, segs):
    xb = x_ref[...].astype(BF16)
    oi = 0
    for start, width, act, outs in segs:
        z = jnp.dot(xb, w_ref[:, start:start + width], preferred_element_type=F32)
        if act is not None:
            z = z + b_ref[:, start:start + width]
            z = _log_sigmoid(z) if act == "log_sigmoid" else jax.nn.sigmoid(z)
        for dtype, scale in outs:
            out_refs[oi][...] = (z * scale if scale != 1.0 else z).astype(dtype)
            oi += 1


def proj(x, w, bias, segs, tm=512):
    nt, d = x.shape
    n = w.shape[1]
    out_shape, out_specs = [], []
    for _, width, _, outs in segs:
        for dtype, _ in outs:
            out_shape.append(jax.ShapeDtypeStruct((nt, width), dtype))
            out_specs.append(pl.BlockSpec((tm, width), lambda i: (i, 0)))
    return pl.pallas_call(
        functools.partial(_proj_kernel, segs=segs),
        grid=(nt // tm,),
        in_specs=[pl.BlockSpec((tm, d), lambda i: (i, 0)),
                  pl.BlockSpec((d, n), lambda i: (0, 0)),
                  pl.BlockSpec((1, n), lambda i: (0, 0))],
        out_specs=out_specs,
        out_shape=out_shape,
        compiler_params=_cparams(("parallel",)),
        name="proj",
    )(x, w, bias)


def _layer_norm(y, g, b):
    mu = jnp.mean(y, axis=-1, keepdims=True)
    yc = y - mu
    var = jnp.mean(yc * yc, axis=-1, keepdims=True)
    return yc * lax.rsqrt(var + LN_EPS) * g + b


def _top2(logits, n_valid):
    lane = lax.broadcasted_iota(jnp.int32, logits.shape, 1)
    l = jnp.where(lane < n_valid, logits, -jnp.inf)
    m1 = jnp.max(l, axis=-1, keepdims=True)
    i1 = jnp.min(jnp.where(l == m1, lane, LANE), axis=-1, keepdims=True)
    l2 = jnp.where(lane == i1, -jnp.inf, l)
    m2 = jnp.max(l2, axis=-1, keepdims=True)
    i2 = jnp.min(jnp.where(l2 == m2, lane, LANE), axis=-1, keepdims=True)
    return m1, i1, m2, i2, lane


def _out_ln_kernel(*refs, n_act, router):
    acts = refs[:n_act]
    w_ref, x_ref, g_ref, b_ref = refs[n_act:n_act + 4]
    rest = refs[n_act + 4:]
    a = acts[0][...]
    if n_act > 1:
        a = a.astype(F32)
        for r in acts[1:]:
            a = a + r[...].astype(F32)
        a = a.astype(BF16)
    m = jnp.dot(a, w_ref[...], preferred_element_type=F32)
    y = _layer_norm(DN_ALPHA * x_ref[...] + m, g_ref[...], b_ref[...])
    if not router:
        rest[0][...] = y
        return
    rw_ref, rb_ref, y_ref, idx_ref, wt_ref = rest
    y_ref[...] = y
    logits = jnp.dot(y, rw_ref[...], preferred_element_type=F32, precision=HI) + rb_ref[...]
    m1, i1, m2, i2, lane = _top2(logits, N_EXPERTS)
    e = jnp.exp(m2 - m1)
    den = 1.0 + e
    idx_ref[...] = jnp.where(lane == 0, i1, jnp.where(lane == 1, i2, 0))
    wt_ref[...] = jnp.where(lane == 0, 1.0 / den, jnp.where(lane == 1, e / den, 0.0))


def out_ln(acts, w, x, g, b, router=None, tm=512):
    nt, d = x.shape
    k = w.shape[0]
    row = lambda i: (i, 0)
    fix = lambda i: (0, 0)
    in_specs = [pl.BlockSpec((tm, k), row) for _ in acts]
    in_specs += [pl.BlockSpec((k, d), fix), pl.BlockSpec((tm, d), row),
                 pl.BlockSpec((1, d), fix), pl.BlockSpec((1, d), fix)]
    args = list(acts) + [w, x, g, b]
    out_shape = [jax.ShapeDtypeStruct((nt, d), F32)]
    out_specs = [pl.BlockSpec((tm, d), row)]
    if router is not None:
        in_specs += [pl.BlockSpec((d, LANE), fix), pl.BlockSpec((1, LANE), fix)]
        args += list(router)
        out_shape += [jax.ShapeDtypeStruct((nt, LANE), jnp.int32),
                      jax.ShapeDtypeStruct((nt, LANE), F32)]
        out_specs += [pl.BlockSpec((tm, LANE), row), pl.BlockSpec((tm, LANE), row)]
    res = pl.pallas_call(
        functools.partial(_out_ln_kernel, n_act=len(acts), router=router is not None),
        grid=(nt // tm,),
        in_specs=in_specs, out_specs=out_specs, out_shape=out_shape,
        compiler_params=_cparams(("parallel",)),
        name="out_ln",
    )(*args)
    return res if router is not None else res[0]


def _ffn_kernel(x_ref, wg_ref, wu_ref, wd_ref, g_ref, b_ref, o_ref, xb_ref, acc_ref):
    c = pl.program_id(1)

    @pl.when(c == 0)
    def _():
        xb_ref[...] = x_ref[...].astype(BF16)
        acc_ref[...] = jnp.zeros_like(acc_ref)

    xb = xb_ref[...]
    gt = jnp.dot(xb, wg_ref[...], preferred_element_type=F32)
    up = jnp.dot(xb, wu_ref[...], preferred_element_type=F32)
    h = (gt * jax.nn.sigmoid(gt) * up).astype(BF16)
    acc_ref[...] += jnp.dot(h, wd_ref[...], preferred_element_type=F32)

    @pl.when(c == pl.num_programs(1) - 1)
    def _():
        o_ref[...] = _layer_norm(DN_ALPHA * x_ref[...] + acc_ref[...], g_ref[...], b_ref[...])


def ffn_ln(x, w_gu, w_down, g, b, tm=1024, tf=256):
    nt, d = x.shape
    dff = w_down.shape[0]
    nc = dff // tf
    return pl.pallas_call(
        _ffn_kernel,
        grid=(nt // tm, nc),
        in_specs=[pl.BlockSpec((tm, d), lambda i, c: (i, 0)),
                  pl.BlockSpec((d, tf), lambda i, c: (0, c)),
                  pl.BlockSpec((d, tf), lambda i, c: (0, nc + c)),
                  pl.BlockSpec((tf, d), lambda i, c: (c, 0)),
                  pl.BlockSpec((1, d), lambda i, c: (0, 0)),
                  pl.BlockSpec((1, d), lambda i, c: (0, 0))],
        out_specs=pl.BlockSpec((tm, d), lambda i, c: (i, 0)),
        out_shape=jax.ShapeDtypeStruct((nt, d), F32),
        scratch_shapes=[pltpu.VMEM((tm, d), BF16), pltpu.VMEM((tm, d), F32)],
        compiler_params=_cparams(("parallel", "arbitrary")),
        name="ffn_ln",
    )(x, w_gu, w_gu, w_down, g, b)


MOE_ROWS = 512
MOE_TF = 512
ROW_DMA_UNROLL = 8


def _row_dma_params():
    return pltpu.CompilerParams(dimension_semantics=("arbitrary",), vmem_limit_bytes=VMEM_LIMIT,
                                has_side_effects=True, disable_bounds_checks=True)


def _dispatch_kernel(pos_ref, x_ref, xs_hbm, sem, *, tm):
    base = pl.program_id(0) * (2 * tm)

    def row_copy(i, k):
        return pltpu.make_async_copy(x_ref.at[pl.ds(i, 1)],
                                     xs_hbm.at[pl.ds(pos_ref[base + 2 * i + k], 1)], sem)

    def start(i, carry):
        row_copy(i, 0).start()
        row_copy(i, 1).start()
        return carry

    def wait(i, carry):
        row_copy(i, 0).wait()
        row_copy(i, 1).wait()
        return carry

    lax.fori_loop(0, tm, start, 0, unroll=ROW_DMA_UNROLL // 2)
    lax.fori_loop(0, tm, wait, 0, unroll=ROW_DMA_UNROLL // 2)


def moe_dispatch(x, pos, tm=512):
    nt, d = x.shape
    return pl.pallas_call(
        functools.partial(_dispatch_kernel, tm=tm),
        grid_spec=pltpu.PrefetchScalarGridSpec(
            num_scalar_prefetch=1, grid=(nt // tm,),
            in_specs=[pl.BlockSpec((tm, d), lambda i, p: (i, 0))],
            out_specs=pl.BlockSpec(memory_space=pl.ANY),
            scratch_shapes=[pltpu.SemaphoreType.DMA(())]),
        out_shape=jax.ShapeDtypeStruct((2 * nt, d), F32),
        compiler_params=_row_dma_params(),
        name="moe_dispatch",
    )(pos, x)


def _moe_kernel(tile_ref, exp_ref, lo_ref, hi_ref, first_ref,
                xs_ref, wg_ref, wu_ref, wd_ref, y_ref, xb_ref, acc_ref):
    w = pl.program_id(0)
    c = pl.program_id(1)
    lo = lo_ref[w]
    hi = hi_ref[w]

    @pl.when(hi > lo)
    def _():
        @pl.when(c == 0)
        def _():
            xb_ref[...] = xs_ref[...].astype(BF16)
            acc_ref[...] = jnp.zeros_like(acc_ref)

        xb = xb_ref[...]
        gt = jnp.dot(xb, wg_ref[...], preferred_element_type=F32)
        up = jnp.dot(xb, wu_ref[...], preferred_element_type=F32)
        h = (gt * jax.nn.sigmoid(gt) * up).astype(BF16)
        acc_ref[...] += jnp.dot(h, wd_ref[...], preferred_element_type=F32)

        @pl.when(c == pl.num_programs(1) - 1)
        def _():
            rows = lax.broadcasted_iota(jnp.int32, acc_ref.shape, 0)
            mine = (rows >= lo) & (rows < hi)

            @pl.when(first_ref[w] == 1)
            def _():
                y_ref[...] = jnp.where(mine, acc_ref[...], 0.0)

            @pl.when(first_ref[w] == 0)
            def _():
                y_ref[...] = jnp.where(mine, acc_ref[...], y_ref[...])


def moe_experts(xs, meta, w_gu, w_down, layer):
    r, d = xs.shape
    n_items = meta[0].shape[0]
    nc = D_FF_EXPERT // MOE_TF
    return pl.pallas_call(
        _moe_kernel,
        grid_spec=pltpu.PrefetchScalarGridSpec(
            num_scalar_prefetch=5, grid=(n_items, nc),
            in_specs=[
                pl.BlockSpec((MOE_ROWS, d), lambda w, c, t, e, lo, hi, f: (t[w], 0)),
                pl.BlockSpec((None, None, d, MOE_TF),
                             lambda w, c, t, e, lo, hi, f: (layer, e[w], 0, c)),
                pl.BlockSpec((None, None, d, MOE_TF),
                             lambda w, c, t, e, lo, hi, f: (layer, e[w], 0, nc + c)),
                pl.BlockSpec((None, None, MOE_TF, d),
                             lambda w, c, t, e, lo, hi, f: (layer, e[w], c, 0)),
            ],
            out_specs=pl.BlockSpec((MOE_ROWS, d), lambda w, c, t, e, lo, hi, f: (t[w], 0)),
            scratch_shapes=[pltpu.VMEM((MOE_ROWS, d), BF16), pltpu.VMEM((MOE_ROWS, d), F32)]),
        out_shape=jax.ShapeDtypeStruct((r, d), F32),
        compiler_params=_cparams(("arbitrary", "arbitrary")),
        name="moe_experts",
    )(*meta, xs, w_gu, w_gu, w_down)


def _combine_kernel(pos_ref, y_hbm, wt_ref, x_ref, g_ref, b_ref, o_ref, buf_ref, sem, *, tm):
    base = pl.program_id(0) * (2 * tm)

    def row_copy(i, k):
        return pltpu.make_async_copy(y_hbm.at[pl.ds(pos_ref[base + 2 * i + k], 1)],
                                     buf_ref.at[k, pl.ds(i, 1)], sem)

    def start(i, carry):
        row_copy(i, 0).start()
        row_copy(i, 1).start()
        return carry

    def wait(i, carry):
        row_copy(i, 0).wait()
        row_copy(i, 1).wait()
        return carry

    lax.fori_loop(0, tm, start, 0, unroll=ROW_DMA_UNROLL // 2)
    lax.fori_loop(0, tm, wait, 0, unroll=ROW_DMA_UNROLL // 2)
    wt = wt_ref[...]
    y = wt[:, 0:1] * buf_ref[0] + wt[:, 1:2] * buf_ref[1]
    o_ref[...] = _layer_norm(DN_ALPHA * x_ref[...] + y, g_ref[...], b_ref[...])


def moe_combine_ln(y, pos, wt, x, g, b, tm=512):
    nt, d = x.shape
    return pl.pallas_call(
        functools.partial(_combine_kernel, tm=tm),
        grid_spec=pltpu.PrefetchScalarGridSpec(
            num_scalar_prefetch=1, grid=(nt // tm,),
            in_specs=[pl.BlockSpec(memory_space=pl.ANY),
                      pl.BlockSpec((tm, LANE), lambda i, p: (i, 0)),
                      pl.BlockSpec((tm, d), lambda i, p: (i, 0)),
                      pl.BlockSpec((1, d), lambda i, p: (0, 0)),
                      pl.BlockSpec((1, d), lambda i, p: (0, 0))],
            out_specs=pl.BlockSpec((tm, d), lambda i, p: (i, 0)),
            scratch_shapes=[pltpu.VMEM((2, tm, d), F32), pltpu.SemaphoreType.DMA(())]),
        out_shape=jax.ShapeDtypeStruct((nt, d), F32),
        compiler_params=_row_dma_params(),
        name="moe_combine_ln",
    )(pos, y, wt, x, g, b)


def _moe_plan(idx):
    nt = idx.shape[0]
    r = 2 * nt
    e_flat = idx[:, :2].reshape(r)
    experts = jnp.arange(N_EXPERTS, dtype=jnp.int32)
    onehot = (e_flat[:, None] == experts[None, :]).astype(jnp.int32)
    csum = jnp.cumsum(onehot, axis=0)
    cnt = csum[-1]
    offs = jnp.cumsum(cnt) - cnt
    pos = jnp.sum(onehot * (csum - 1 + offs[None, :]), axis=1).astype(jnp.int32)
    n_tiles = r // MOE_ROWS
    n_items = n_tiles + N_EXPERTS - 1
    t_start = offs // MOE_ROWS
    t_end = (offs + cnt + MOE_ROWS - 1) // MOE_ROWS
    n_e = jnp.where(cnt > 0, t_end - t_start, 0)
    item_end = jnp.cumsum(n_e)
    item_start = item_end - n_e
    total = item_end[-1]
    w = jnp.arange(n_items, dtype=jnp.int32)
    e_w = jnp.minimum(jnp.sum((item_end[None, :] <= w[:, None]).astype(jnp.int32), axis=1),
                      N_EXPERTS - 1)
    tile_w = t_start[e_w] + (w - item_start[e_w])
    valid = w < total
    lo = jnp.maximum(offs[e_w], tile_w * MOE_ROWS) - tile_w * MOE_ROWS
    hi = jnp.minimum(offs[e_w] + cnt[e_w], (tile_w + 1) * MOE_ROWS) - tile_w * MOE_ROWS
    last = jnp.maximum(total - 1, 0)
    tile_w = jnp.where(valid, tile_w, tile_w[last]).astype(jnp.int32)
    e_w = jnp.where(valid, e_w, e_w[last]).astype(jnp.int32)
    lo = jnp.where(valid, lo, 0).astype(jnp.int32)
    hi = jnp.where(valid, hi, 0).astype(jnp.int32)
    prev = jnp.concatenate([jnp.full((1,), -1, jnp.int32), tile_w[:-1]])
    first = (valid & (tile_w != prev)).astype(jnp.int32)
    return pos, (tile_w, e_w, lo, hi, first)


def moe_ln(x, idx, wt, w_gu, w_down, layer, g, b):
    pos, meta = _moe_plan(idx)
    xs = moe_dispatch(x, pos)
    y = moe_experts(xs, meta, w_gu, w_down, layer)
    return moe_combine_ln(y, pos, wt, x, g, b)


def _gate_pair(g_ref, col0, lane):
    sel = lax.broadcasted_iota(jnp.int32, (LANE, LANE), 0) == (
        col0 + (lax.broadcasted_iota(jnp.int32, (LANE, LANE), 1) >= HEAD_DIM).astype(jnp.int32))
    return jnp.dot(g_ref[...], sel.astype(F32), preferred_element_type=F32, precision=HI)


def _pair_attn_kernel(*refs, mode, t):
    if mode == "fox":
        q_ref, k_ref, v_ref, c_ref, o_ref, kx_ref, vx_ref = refs
    elif mode == "sel":
        q_ref, k_ref, v_ref, sm_ref, e_ref, g_ref, o_ref, vx_ref = refs
    else:
        q_ref, k_ref, v_ref, g_ref, o_ref, vx_ref = refs
    j = pl.program_id(1)
    qi = pl.program_id(2)
    seq = k_ref.shape[0]

    def spare_index(shape, half):
        lane = lax.broadcasted_iota(jnp.int32, shape, 1)
        return lane - HEAD_DIM if half == 0 else lane, (lane < HEAD_DIM) == (half == 0)

    def split3(c):
        hi = c.astype(BF16)
        r1 = c - hi.astype(F32)
        mid = r1.astype(BF16)
        lo = (r1 - mid.astype(F32)).astype(BF16)
        return hi, mid, lo

    def with_terms(x, half, terms):
        sp, own = spare_index(x.shape, half)
        out = jnp.zeros(x.shape, F32)
        for i, term in reversed(list(enumerate(terms))):
            out = jnp.where(sp == i, term.astype(F32), out)
        return jnp.where(own, x.astype(F32), out).astype(BF16)

    one = jnp.ones((), BF16)

    @pl.when(qi == 0)
    def _():
        for half in range(2):
            _, own = spare_index((seq, LANE), half)
            vx_ref[half] = jnp.where(own, v_ref[...].astype(F32), 1.0).astype(BF16)
            if mode == "fox":
                hi, mid, lo = split3(c_ref[:, half:half + 1])
                kx_ref[half] = with_terms(k_ref[...], half, [-hi, -mid, -lo, one, one, one])

    q = q_ref[...]
    lane = lax.broadcasted_iota(jnp.int32, (t, LANE), 1)
    row = lax.broadcasted_iota(jnp.int32, (t, t), 0)
    col = lax.broadcasted_iota(jnp.int32, (t, t), 1)
    causal = col <= row
    sm = sm_ref[...] if mode == "sel" else None
    halves = []
    for half in range(2):
        if mode == "fox":
            hi, mid, lo = split3(c_ref[pl.ds(pl.multiple_of(qi * t, t), t), half:half + 1])
            qh = with_terms(q, half, [one, one, one, hi, mid, lo])
        else:
            qh = with_terms(q, half, [])

        def step(kb, carry, kind):
            k0 = pl.multiple_of(kb * t, t)
            k = kx_ref[half, pl.ds(k0, t), :] if mode == "fox" else k_ref[pl.ds(k0, t), :]
            s = _dot_nt(qh, k)
            valid = None
            if mode == "sel":
                valid = jnp.dot(sm, e_ref[kb], preferred_element_type=F32) > 0.5
            if kind == "diag":
                valid = causal if valid is None else (valid & causal)
            if kind == "band":
                valid = (col >= row) & (qi > 0)
            m, acc = carry
            if valid is not None:
                s = jnp.where(valid, s, NEG)
            m_new = jnp.maximum(m, jnp.max(s, axis=1, keepdims=True))
            p = jnp.exp(s - m_new)
            if valid is not None:
                p = jnp.where(valid, p, 0.0)
            acc = jnp.exp(m - m_new) * acc + jnp.dot(
                p.astype(BF16), vx_ref[half, pl.ds(k0, t), :], preferred_element_type=F32)
            return m_new, acc

        carry = (jnp.full((t, 1), NEG, F32), jnp.zeros((t, LANE), F32))
        if mode == "win":
            carry = step(jnp.maximum(qi - 1, 0), carry, "band")
        else:
            carry = lax.fori_loop(0, qi, lambda kb, c: step(kb, c, "full"), carry)
        m, acc = step(qi, carry, "diag")
        halves.append(acc / jnp.maximum(pltpu.roll(acc, HEAD_DIM, 1), 1e-30))
    o = jnp.where(lane < HEAD_DIM, halves[0], halves[1])
    if mode != "fox":
        branch = 1 if mode == "sel" else 2
        o = o * _gate_pair(g_ref, branch * N_HEADS + 2 * j, lane)
    o_ref[...] = o.astype(o_ref.dtype)


def pair_attention(mode, q, k, v, kcol, vcol, bsz, seq, extra):
    t = ATT_T
    nq = seq // t
    n_pairs = N_HEADS // 2
    in_specs = [pl.BlockSpec((t, LANE), lambda b, j, i: (b * nq + i, j)),
                pl.BlockSpec((seq, LANE), lambda b, j, i: (b, kcol(j))),
                pl.BlockSpec((seq, LANE), lambda b, j, i: (b, vcol(j)))]
    scratch = [pltpu.VMEM((2, seq, LANE), BF16)]
    if mode == "fox":
        in_specs.append(pl.BlockSpec((None, None, seq, 2), lambda b, j, i: (b, j, 0, 0)))
        scratch = scratch * 2
    elif mode == "sel":
        in_specs += [pl.BlockSpec((t, LANE), lambda b, j, i: (b * nq + i, 0)),
                     pl.BlockSpec((None, nq, LANE, t), lambda b, j, i: (j // 2, 0, 0, 0)),
                     pl.BlockSpec((t, LANE), lambda b, j, i: (b * nq + i, 0))]
    else:
        in_specs.append(pl.BlockSpec((t, LANE), lambda b, j, i: (b * nq + i, 0)))
    return pl.pallas_call(
        functools.partial(_pair_attn_kernel, mode=mode, t=t),
        grid=(bsz, n_pairs, nq),
        in_specs=in_specs,
        out_specs=pl.BlockSpec((t, LANE), lambda b, j, i: (b * nq + i, j)),
        out_shape=jax.ShapeDtypeStruct((bsz * seq, D_MODEL), BF16),
        scratch_shapes=scratch,
        compiler_params=_cparams(("parallel", "parallel", "arbitrary")),
        name="attn_" + mode,
    )(q, k, v, *extra)


def _gelu(x):
    return 0.5 * x * (1.0 + jnp.tanh(0.7978845608028654 * (x + 0.044715 * (x * x * x))))


def _compress_body(xs, w1_ref, b1_ref, pe_ref, w2d_ref, w2s_ref, dup_ref, nat_ref):
    n_chunk = xs[0].shape[0] // CMP_STRIDE
    n_rp = CMP_STRIDE // 2
    for which in range(2):
        nat = jnp.zeros((n_chunk, KV_W), F32)
        for cb in range(2):
            x_ref = xs[2 * which + cb]
            top = jnp.zeros((n_chunk, 2 * CMP_HIDDEN), F32)
            bot = jnp.zeros((n_chunk, 2 * CMP_HIDDEN), F32)
            for rp in range(n_rp):
                lhs = jnp.concatenate(
                    [x_ref[pl.ds(2 * rp, n_chunk, stride=CMP_STRIDE), :],
                     x_ref[pl.ds(2 * rp + 1, n_chunk, stride=CMP_STRIDE), :]], axis=1)
                i_top = which * 2 * n_rp + rp
                i_bot = i_top + n_rp
                top = top + jnp.dot((lhs + pe_ref[i_top]).astype(BF16), w1_ref[i_top],
                                    preferred_element_type=F32)
                bot = bot + jnp.dot((lhs + pe_ref[i_bot]).astype(BF16), w1_ref[i_bot],
                                    preferred_element_type=F32)
            hid = _gelu(top + pltpu.roll(bot, n_chunk - 1, 0) + b1_ref[which]).astype(BF16)
            for gg in range(2):
                g = 2 * cb + gg
                h = hid[:, gg * CMP_HIDDEN:(gg + 1) * CMP_HIDDEN]
                dup_ref[which, g] = jnp.dot(h, w2d_ref[which],
                                            preferred_element_type=F32).astype(dup_ref.dtype)
                nat = nat + jnp.dot(h, w2s_ref[which, g], preferred_element_type=F32)
        nat_ref[which] = nat.astype(nat_ref.dtype)


def _compress_kernel(x0_ref, x1_ref, x2_ref, x3_ref, *rest):
    _compress_body((x0_ref, x1_ref, x2_ref, x3_ref), *rest)


def _compress_pages_kernel(pt_ref, *refs, n_pages):
    pages = refs[:n_pages]
    rest = refs[n_pages:n_pages + 7]
    cols = refs[n_pages + 7:]
    for p in range(n_pages):
        x = pages[p][...].T
        for c in range(4):
            cols[c][p * PAGE:(p + 1) * PAGE, :] = x[:, c * LANE:(c + 1) * LANE]
    _compress_body(cols, *rest)


def _compress_out(n_seq, n_chunk):
    specs = [pl.BlockSpec((None, 2, NSA_G, n_chunk, LANE), lambda s, *_: (s, 0, 0, 0, 0)),
             pl.BlockSpec((None, 2, n_chunk, KV_W), lambda s, *_: (s, 0, 0, 0))]
    shapes = [jax.ShapeDtypeStruct((n_seq, 2, NSA_G, n_chunk, LANE), BF16),
              jax.ShapeDtypeStruct((n_seq, 2, n_chunk, KV_W), BF16)]
    return specs, shapes


def compress_pages(pt, cache_t, n_seq, n_pages, w1, b1, pe, w2d, w2s):
    seq = n_pages * PAGE
    fix3 = lambda s, pt: (0, 0, 0)

    def page_spec(p):
        return pl.BlockSpec((None, 2 * KV_W, PAGE), lambda s, pt: (pt[s * n_pages + p], 0, 0))

    out_specs, out_shape = _compress_out(n_seq, seq // CMP_STRIDE)
    return pl.pallas_call(
        functools.partial(_compress_pages_kernel, n_pages=n_pages),
        grid_spec=pltpu.PrefetchScalarGridSpec(
            num_scalar_prefetch=1, grid=(n_seq,),
            in_specs=[page_spec(p) for p in range(n_pages)] + [
                pl.BlockSpec(w1.shape, fix3), pl.BlockSpec(b1.shape, fix3),
                pl.BlockSpec(pe.shape, fix3), pl.BlockSpec(w2d.shape, fix3),
                pl.BlockSpec(w2s.shape, lambda s, pt: (0, 0, 0, 0))],
            out_specs=out_specs,
            scratch_shapes=[pltpu.VMEM((seq, LANE), F32) for _ in range(4)]),
        out_shape=out_shape,
        compiler_params=_cparams(("parallel",)),
        name="nsa_compress_pages",
    )(pt, *([cache_t] * n_pages), w1, b1, pe, w2d, w2s)


def compress(x, n_seq, seq, w1, b1, pe, w2d, w2s):
    fix = lambda s: (0, 0, 0)
    col = lambda c: pl.BlockSpec((seq, LANE), lambda s: (s, c))
    out_specs, out_shape = _compress_out(n_seq, seq // CMP_STRIDE)
    return pl.pallas_call(
        _compress_kernel,
        grid=(n_seq,),
        in_specs=[col(0), col(1), col(2), col(3),
                  pl.BlockSpec(w1.shape, fix), pl.BlockSpec(b1.shape, fix),
                  pl.BlockSpec(pe.shape, fix), pl.BlockSpec(w2d.shape, fix),
                  pl.BlockSpec(w2s.shape, lambda s: (0, 0, 0, 0))],
        out_specs=out_specs,
        out_shape=out_shape,
        compiler_params=_cparams(("parallel",)),
        name="nsa_compress",
    )(x, x, x, x, w1, b1, pe, w2d, w2s)


def _block_rank(score, blk, group):
    rank = jnp.zeros(score.shape, jnp.int32)
    for k in range(1, group):
        if group == LANE:
            other = pltpu.roll(score, k, 1)
            io = jnp.where(blk >= k, blk - k, blk - k + group)
        else:
            wrap = blk >= k
            other = jnp.where(wrap, pltpu.roll(score, k, 1), pltpu.roll(score, k + LANE - group, 1))
            io = jnp.where(wrap, blk - k, blk - k + group)
        beats = (other > score) | ((other == score) & (io < blk))
        rank = rank + beats.astype(jnp.int32)
    return rank


def _cmp_prompt_kernel(q_ref, kc_ref, vc_ref, g_ref, ov_ref, oc_ref, sm_ref, *, t):
    qi = pl.program_id(1)
    lane = lax.broadcasted_iota(jnp.int32, (t, LANE), 1)
    pos = qi * t + lax.broadcasted_iota(jnp.int32, (t, LANE), 0)
    n_cmp = kc_ref.shape[1] - 1
    valid = (lane * CMP_STRIDE + CMP_BLOCK - 1 <= pos) & (lane < n_cmp)
    gates = g_ref[...]
    imp = jnp.zeros((t, LANE), F32)
    for g in range(NSA_G):
        kc = kc_ref[g]
        vc = vc_ref[g]
        psum = jnp.zeros((t, LANE), F32)
        for pr in range(2):
            pair = 2 * g + pr
            q = q_ref[:, pair * LANE:(pair + 1) * LANE]
            outs = []
            for half in range(2):
                qh = jnp.where((lane < HEAD_DIM) == (half == 0), q, jnp.zeros_like(q))
                s = jnp.where(valid, _dot_nt(qh, kc), NEG)
                m = jnp.max(s, axis=1, keepdims=True)
                p = jnp.where(valid, jnp.exp(s - m), 0.0)
                p = p / jnp.maximum(jnp.sum(p, axis=1, keepdims=True), 1e-30)
                psum = psum + p
                outs.append(jnp.dot(p.astype(BF16), vc, preferred_element_type=F32))
            h0 = 2 * pair
            gate = jnp.where(lane < HEAD_DIM, gates[:, h0:h0 + 1], gates[:, h0 + 1:h0 + 2])
            o = jnp.where(lane < HEAD_DIM, outs[0], outs[1]) * gate
            oc_ref[:, pair * LANE:(pair + 1) * LANE] = o.astype(oc_ref.dtype)
        imp = imp + jnp.dot(psum, ov_ref[g], preferred_element_type=F32, precision=HI)
    n_blk = LANE // NSA_G
    blk = lane % n_blk
    cur = pos // SEL_BLOCK
    forced = (blk == 0) | (cur - blk < N_LOCAL_BLOCKS)
    score = jnp.where(blk <= cur, imp + jnp.where(forced, FORCE_BONUS, 0.0), NEG)
    rank = _block_rank(score, blk, n_blk)
    sm_ref[...] = jnp.where(rank < N_SELECT, 1.0, 0.0).astype(sm_ref.dtype)


def cmp_prompt(q, kc_dup, vc_dup, gates, ov, bsz, seq):
    t = ATT_T
    nq = seq // t
    n_chunk = kc_dup.shape[2]
    return pl.pallas_call(
        functools.partial(_cmp_prompt_kernel, t=t),
        grid=(bsz, nq),
        in_specs=[pl.BlockSpec((t, D_MODEL), lambda b, i: (b * nq + i, 0)),
                  pl.BlockSpec((None, NSA_G, n_chunk, LANE), lambda b, i: (b, 0, 0, 0)),
                  pl.BlockSpec((None, NSA_G, n_chunk, LANE), lambda b, i: (b, 0, 0, 0)),
                  pl.BlockSpec((t, LANE), lambda b, i: (b * nq + i, 0)),
                  pl.BlockSpec((NSA_G, LANE, LANE), lambda b, i: (0, 0, 0))],
        out_specs=[pl.BlockSpec((t, D_MODEL), lambda b, i: (b * nq + i, 0)),
                   pl.BlockSpec((t, LANE), lambda b, i: (b * nq + i, 0))],
        out_shape=[jax.ShapeDtypeStruct((bsz * seq, D_MODEL), BF16),
                   jax.ShapeDtypeStruct((bsz * seq, LANE), BF16)],
        compiler_params=_cparams(("parallel", "arbitrary")),
        name="nsa_cmp_prompt",
    )(q, kc_dup, vc_dup, gates, ov)


def _softplus(x):
    return jnp.maximum(x, 0.0) + jnp.log1p(jnp.exp(-jnp.abs(x)))


def _rg_gates_and_scan(u3, prev3, gb, cw_ref, cb_ref, wg_ref, gab_ref, gxb_ref, lam_ref):
    nt8, _, c = u3.shape
    rows = nt8 * SUB
    rowi = lax.broadcasted_iota(jnp.int32, u3.shape, 1)

    def shifted(s):
        return jnp.where(rowi >= s, pltpu.roll(u3, s, 1), pltpu.roll(prev3, s, 1))

    cx3 = (cb_ref[...] + cw_ref[0:1, :] * shifted(3) + cw_ref[1:2, :] * shifted(2)
           + cw_ref[2:3, :] * shifted(1) + cw_ref[3:4, :] * u3)
    cx = cx3.reshape(rows, c)
    z = jnp.dot(cx.astype(BF16), wg_ref[...], preferred_element_type=F32)
    r = jax.nn.sigmoid(z[:, :c] + gab_ref[...])
    ig = jax.nn.sigmoid(z[:, c:] + gxb_ref[...])
    log_a = -RG_C * r * _softplus(-lam_ref[...])
    a = jnp.exp(log_a)
    b = jnp.sqrt(1.0 - jnp.exp(2.0 * log_a)) * (ig * cx)
    a3 = a.reshape(nt8, SUB, c)
    b3 = b.reshape(nt8, SUB, c)
    for s in (1, 2, 4):
        keep = rowi >= s
        b3 = jnp.where(keep, a3 * pltpu.roll(b3, s, 1) + b3, b3)
        a3 = jnp.where(keep, a3 * pltpu.roll(a3, s, 1), a3)
    return a3, b3, _gelu(gb)


def _rg_prompt_kernel(gb_ref, u_ref, cw_ref, cb_ref, wg_ref, gab_ref, gxb_ref, lam_ref,
                      y_ref, hl_ref, cu_ref, ch_ref, a_ref, b_ref, h_ref):
    rows, c = u_ref.shape
    nt8 = rows // SUB

    @pl.when(pl.program_id(1) == 0)
    def _():
        cu_ref[...] = jnp.zeros_like(cu_ref)
        ch_ref[...] = jnp.zeros_like(ch_ref)

    u3 = u_ref[...].reshape(nt8, SUB, c)
    prev3 = jnp.concatenate([cu_ref[...][None], u3[:nt8 - 1]], axis=0)
    a3, b3, gelu_gb = _rg_gates_and_scan(u3, prev3, gb_ref[...], cw_ref, cb_ref, wg_ref,
                                         gab_ref, gxb_ref, lam_ref)
    a_ref[...] = a3
    b_ref[...] = b3

    def body(j, hb):
        h = a_ref[j] * hb + b_ref[j]
        h_ref[j] = h
        return jnp.broadcast_to(h[SUB - 1:SUB, :], (SUB, c))

    hb = lax.fori_loop(0, nt8, body, ch_ref[...])
    ch_ref[...] = hb
    cu_ref[...] = u3[nt8 - 1]
    hl_ref[...] = hb
    y_ref[...] = (gelu_gb * h_ref[...].reshape(rows, c)).astype(y_ref.dtype)


def _rg_sample_kernel(gb_ref, u_ref, prev_ref, h0_ref, cw_ref, cb_ref, wg_ref, gab_ref, gxb_ref,
                      lam_ref, y_ref, h_ref):
    rows, c = u_ref.shape
    nt8 = rows // SUB
    u3 = u_ref[...].reshape(nt8, SUB, c)
    prev3 = prev_ref[...].reshape(nt8, SUB, c)
    a3, b3, gelu_gb = _rg_gates_and_scan(u3, prev3, gb_ref[...], cw_ref, cb_ref, wg_ref,
                                         gab_ref, gxb_ref, lam_ref)
    h = (a3 * h0_ref[...].reshape(nt8, SUB, c) + b3).reshape(rows, c)
    h_ref[...] = h
    y_ref[...] = (gelu_gb * h).astype(y_ref.dtype)


def _rg_weight_specs(n_grid):
    c = D_RNNP
    fix = (lambda b, i: (0, 0)) if n_grid == 2 else (lambda i: (0, 0))
    return [pl.BlockSpec((CONV_W, c), fix), pl.BlockSpec((1, c), fix),
            pl.BlockSpec((c, 2 * c), fix), pl.BlockSpec((1, c), fix),
            pl.BlockSpec((1, c), fix), pl.BlockSpec((1, c), fix)]


def rg_prompt(gb, u, weights, bsz, seq, rows=256):
    c = D_RNNP
    nc = seq // rows
    blk = lambda b, i: (b * nc + i, 0)
    return pl.pallas_call(
        _rg_prompt_kernel,
        grid=(bsz, nc),
        in_specs=[pl.BlockSpec((rows, c), blk), pl.BlockSpec((rows, c), blk)] + _rg_weight_specs(2),
        out_specs=[pl.BlockSpec((rows, c), blk), pl.BlockSpec((SUB, c), lambda b, i: (b, 0))],
        out_shape=[jax.ShapeDtypeStruct((bsz * seq, c), BF16),
                   jax.ShapeDtypeStruct((bsz * SUB, c), F32)],
        scratch_shapes=[pltpu.VMEM((SUB, c), F32), pltpu.VMEM((SUB, c), F32),
                        pltpu.VMEM((rows // SUB, SUB, c), F32),
                        pltpu.VMEM((rows // SUB, SUB, c), F32),
                        pltpu.VMEM((rows // SUB, SUB, c), F32)],
        compiler_params=_cparams(("parallel", "arbitrary")),
        name="rg_prompt",
    )(gb, u, *weights)


def rg_sample(gb, u, row0, prev, h0, weights, rows=256):
    c = D_RNNP
    n = prev.shape[0]
    while n % rows or row0 % rows:
        rows -= SUB
    blk = lambda i: (i, 0)
    off = lambda i: (row0 // rows + i, 0)
    return pl.pallas_call(
        _rg_sample_kernel,
        grid=(n // rows,),
        in_specs=[pl.BlockSpec((rows, c), off)] * 2 + [pl.BlockSpec((rows, c), blk)] * 2
        + _rg_weight_specs(1),
        out_specs=[pl.BlockSpec((rows, c), blk), pl.BlockSpec((rows, c), blk)],
        out_shape=[jax.ShapeDtypeStruct((n, c), BF16), jax.ShapeDtypeStruct((n, c), F32)],
        compiler_params=_cparams(("parallel",)),
        name="rg_sample",
    )(gb, u, prev, h0, *weights)


def _rows_of_heads(x16):
    n = x16.shape[1]
    return jnp.broadcast_to(x16[:, None, :], (N_HEADS, SUB, n)).reshape(N_HEADS * SUB, n)


def _pad_new_rows(x):
    return jnp.concatenate([x, jnp.zeros((PAGE - SUB, x.shape[1]), x.dtype)], axis=0)


def _logf_suffix_kernel(x_ref, exc_ref, tot_ref):
    n, h, _ = x_ref.shape
    x = x_ref[...].reshape(n * h, PAGE)
    r = lax.broadcasted_iota(jnp.int32, (PAGE, PAGE), 0)
    c = lax.broadcasted_iota(jnp.int32, (PAGE, PAGE), 1)
    exc = jnp.dot(x, (r > c).astype(F32), preferred_element_type=F32, precision=HI)
    tot = jnp.dot(x, jnp.ones((PAGE, PAGE), F32), preferred_element_type=F32, precision=HI)
    exc_ref[...] = exc.reshape(n, h, PAGE)
    tot_ref[...] = tot.reshape(n, h, PAGE)


def logf_suffix(logf_t):
    pool, h, _ = logf_t.shape
    n = 64 if pool % 64 == 0 else SUB
    spec = pl.BlockSpec((n, h, PAGE), lambda i: (i, 0, 0))
    return pl.pallas_call(
        _logf_suffix_kernel,
        grid=(pool // n,),
        in_specs=[spec], out_specs=[spec, spec],
        out_shape=[jax.ShapeDtypeStruct(logf_t.shape, F32)] * 2,
        compiler_params=_cparams(("parallel",)),
        name="logf_suffix",
    )(logf_t)


FOX_PAGES_PER_STEP = 8


def _fox_sample_kernel(pt_ref, q_ref, lfn_ref, kvn_ref, *refs):
    npp = FOX_PAGES_PER_STEP
    pg_refs = refs[:npp]
    exc_refs, tot_refs = refs[npp:2 * npp], refs[2 * npp:3 * npp]
    o_ref, m_ref, l_ref, acc_ref, sfx_ref, cq_ref = refs[3 * npp:]
    step = pl.program_id(1)
    rows = N_HEADS * SUB
    q = q_ref[...]

    def update(s, valid, pv):
        if valid is not None:
            s = jnp.where(valid, s, NEG)
        m_prev = m_ref[...]
        m_new = jnp.maximum(m_prev, jnp.max(s, axis=1, keepdims=True))
        p = jnp.exp(s - m_new)
        if valid is not None:
            p = jnp.where(valid, p, 0.0)
        alpha = jnp.exp(m_prev - m_new)
        l_ref[...] = alpha * l_ref[...] + jnp.sum(p, axis=1, keepdims=True)
        acc_ref[...] = alpha * acc_ref[...] + pv(p.astype(BF16))
        m_ref[...] = m_new

    @pl.when(step == 0)
    def _():
        row = lax.broadcasted_iota(jnp.int32, (rows, LANE), 0)
        col = lax.broadcasted_iota(jnp.int32, (rows, LANE), 1)
        qpos = row % SUB
        m_ref[...] = jnp.full_like(m_ref, NEG)
        l_ref[...] = jnp.zeros_like(l_ref)
        acc_ref[...] = jnp.zeros_like(acc_ref)
        sfx_ref[...] = jnp.zeros_like(sfx_ref)
        lf = _pad_new_rows(lfn_ref[...])
        tri = (lax.broadcasted_iota(jnp.int32, (LANE, LANE), 0)
               >= lax.broadcasted_iota(jnp.int32, (LANE, LANE), 1)).astype(F32)
        cnew = jnp.dot(tri, lf, preferred_element_type=F32, precision=HI)
        head_of_row = (col == row // SUB).astype(F32)
        c_keys = _dot_nt(head_of_row, cnew, precision=HI)
        cq = jnp.sum(jnp.where(col == qpos, c_keys, 0.0), axis=1, keepdims=True)
        cq_ref[...] = cq
        k_new = _pad_new_rows(kvn_ref[:, 0:D_MODEL]).astype(BF16)
        v_new = _pad_new_rows(kvn_ref[:, D_MODEL:2 * D_MODEL]).astype(BF16)
        s = _dot_nt(q, k_new) + (cq - c_keys)
        update(s, col <= qpos, lambda p: jnp.dot(p, v_new, preferred_element_type=F32))

    @pl.when(step > 0)
    def _():
        run = sfx_ref[...]
        bias = [None] * npp
        for i in reversed(range(npp)):
            bias[i] = _rows_of_heads(exc_refs[i][...] + run)
            run = run + tot_refs[i][...]
        sfx_ref[...] = run
        s = jnp.concatenate(
            [jnp.dot(q, pg_refs[i][0:D_MODEL, :].astype(BF16), preferred_element_type=F32) + bias[i]
             for i in range(npp)], axis=1) + cq_ref[...]
        v_all = jnp.concatenate([pg_refs[i][D_MODEL:2 * D_MODEL, :].astype(BF16)
                                 for i in range(npp)], axis=1)
        update(s, None, lambda p: _dot_nt(p, v_all))

    @pl.when(step == pl.num_programs(1) - 1)
    def _():
        o = acc_ref[...] / jnp.maximum(l_ref[...], 1e-30)
        o3 = o.reshape(N_HEADS, SUB, D_MODEL)
        own = (lax.broadcasted_iota(jnp.int32, o3.shape, 2) // HEAD_DIM
               == lax.broadcasted_iota(jnp.int32, o3.shape, 0))
        o_ref[...] = jnp.sum(jnp.where(own, o3, 0.0), axis=0)


def fox_sample(pt, qbd, lf_new, kv, row0, cache_t, exc, tot, slot, n_seq, n_pages):
    rows = N_HEADS * SUB
    npp = FOX_PAGES_PER_STEP
    assert n_pages % npp == 0

    def page(i):
        return lambda b, s, pt: pt[b * n_pages + n_pages - jnp.maximum(s, 1) * npp + i]

    def cache_spec(i):
        pg = page(i)
        return pl.BlockSpec((None, None, 2 * D_MODEL, LANE), lambda b, s, pt: (slot, pg(b, s, pt), 0, 0))

    def lf_spec(i):
        pg = page(i)
        return pl.BlockSpec((None, N_HEADS, LANE), lambda b, s, pt: (pg(b, s, pt), 0, 0))

    blk0 = row0 // SUB
    in_specs = [pl.BlockSpec((None, rows, D_MODEL), lambda b, s, pt: (b, 0, 0)),
                pl.BlockSpec((SUB, LANE), lambda b, s, pt: (blk0 + b, 0)),
                pl.BlockSpec((SUB, 2 * D_MODEL), lambda b, s, pt: (blk0 + b, 0))]
    in_specs += [cache_spec(i) for i in range(npp)] + [lf_spec(i) for i in range(npp)] * 2
    return pl.pallas_call(
        _fox_sample_kernel,
        grid_spec=pltpu.PrefetchScalarGridSpec(
            num_scalar_prefetch=1, grid=(n_seq, n_pages // npp + 1),
            in_specs=in_specs,
            out_specs=pl.BlockSpec((SUB, D_MODEL), lambda b, s, pt: (b, 0)),
            scratch_shapes=[pltpu.VMEM((rows, 1), F32), pltpu.VMEM((rows, 1), F32),
                            pltpu.VMEM((rows, D_MODEL), F32), pltpu.VMEM((N_HEADS, LANE), F32),
                            pltpu.VMEM((rows, 1), F32)]),
        out_shape=jax.ShapeDtypeStruct((n_seq * SUB, D_MODEL), F32),
        compiler_params=_cparams(("parallel", "arbitrary")),
        name="fox_sample",
    )(pt, qbd, lf_new, kv, *([cache_t] * npp), *([exc] * npp), *([tot] * npp))


def _nsa_sample_kernel(pt_ref, q_ref, g_ref, kv_ref, ov_ref, e_ref, win_ref, new_ref, *refs):
    pages, o_ref = refs[:-1], refs[-1]
    n_pages = len(pages)
    rows = N_HEADS * SUB
    q = q_ref[...]
    row = lax.broadcasted_iota(jnp.int32, (rows, LANE), 0)
    col = lax.broadcasted_iota(jnp.int32, (rows, LANE), 1)
    new_ok = col <= row % SUB
    new = _pad_new_rows(new_ref[:, 2 * KV_W:]).astype(BF16)

    def attend(scores, valids, values):
        scores = [jnp.where(v, s, NEG) for s, v in zip(scores, valids)]
        m = functools.reduce(jnp.maximum, [jnp.max(s, axis=1, keepdims=True) for s in scores])
        ps = [jnp.where(v, jnp.exp(s - m), 0.0) for s, v in zip(scores, valids)]
        den = functools.reduce(lambda a, b: a + b, [jnp.sum(p, axis=1, keepdims=True) for p in ps])
        out = functools.reduce(lambda a, b: a + b, [f(p.astype(BF16)) for f, p in zip(values, ps)])
        return out / jnp.maximum(den, 1e-30), ps, den

    n_cmp = kv_ref.shape[1] - 1
    vc = kv_ref[1]
    o_c, (p,), den = attend([_dot_nt(q, kv_ref[0])], [col < n_cmp],
                            [lambda pb: jnp.dot(pb, vc, preferred_element_type=F32)])
    p = p / jnp.maximum(den, 1e-30)
    psum = jnp.sum(p.reshape(NSA_G, NSA_HPG, SUB, LANE), axis=1).reshape(NSA_G * SUB, LANE)
    imp = jnp.dot(psum, ov_ref[...], preferred_element_type=F32, precision=HI)
    blk = lax.broadcasted_iota(jnp.int32, imp.shape, 1)
    cur = (n_pages * PAGE) // SEL_BLOCK
    forced = (blk == 0) | (cur - blk < N_LOCAL_BLOCKS)
    score = jnp.where(blk <= cur, imp + jnp.where(forced, FORCE_BONUS, 0.0), NEG)
    sel = (_block_rank(score, blk, LANE) < N_SELECT).astype(BF16)
    sel_rows = jnp.broadcast_to(sel.reshape(NSA_G, 1, SUB, LANE),
                                (NSA_G, NSA_HPG, SUB, LANE)).reshape(rows, LANE)
    key_ok = jnp.dot(sel_rows, e_ref[...], preferred_element_type=F32) > 0.5
    wrow = lax.broadcasted_iota(jnp.int32, (rows, WINDOW), 0) % SUB
    wcol = lax.broadcasted_iota(jnp.int32, (rows, WINDOW), 1)
    v_win = win_ref[KV_W:2 * KV_W, :].astype(BF16)
    o_w, _, _ = attend(
        [jnp.dot(q, win_ref[0:KV_W, :].astype(BF16), preferred_element_type=F32),
         _dot_nt(q, new[:, 2 * KV_W:3 * KV_W])],
        [wcol >= wrow, new_ok],
        [lambda pb: _dot_nt(pb, v_win),
         lambda pb: jnp.dot(pb, new[:, 3 * KV_W:4 * KV_W], preferred_element_type=F32)])
    k_all = jnp.concatenate([pg[0:KV_W, :].astype(BF16) for pg in pages], axis=1)
    v_all = jnp.concatenate([pg[KV_W:2 * KV_W, :].astype(BF16) for pg in pages], axis=1)
    past = n_pages * PAGE
    o_s, _, _ = attend(
        [jnp.dot(q, k_all, preferred_element_type=F32), _dot_nt(q, new[:, 0:KV_W])],
        [key_ok[:, :past], new_ok & key_ok[:, past:]],
        [lambda pb: _dot_nt(pb, v_all),
         lambda pb: jnp.dot(pb, new[:, KV_W:2 * KV_W], preferred_element_type=F32)])
    g = g_ref[...]
    o_ref[...] = g[:, 0:1] * o_c + g[:, 1:2] * o_s + g[:, 2:3] * o_w


def nsa_sample(pt, qbd, grow, kv_nat, ov, emat, win_t, kv, row0, cache_t, n_seq, n_pages):
    rows = N_HEADS * SUB
    n_chunk = kv_nat.shape[2]
    blk0 = row0 // SUB

    def page_spec(p):
        return pl.BlockSpec((None, 2 * KV_W, PAGE), lambda b, pt: (pt[b * n_pages + p], 1, 0))

    return pl.pallas_call(
        _nsa_sample_kernel,
        grid_spec=pltpu.PrefetchScalarGridSpec(
            num_scalar_prefetch=1, grid=(n_seq,),
            in_specs=[
                pl.BlockSpec((None, rows, KV_W), lambda b, pt: (b, 0, 0)),
                pl.BlockSpec((None, rows, LANE), lambda b, pt: (b, 0, 0)),
                pl.BlockSpec((None, 2, n_chunk, KV_W), lambda b, pt: (b, 0, 0, 0)),
                pl.BlockSpec((LANE, LANE), lambda b, pt: (0, 0)),
                pl.BlockSpec(emat.shape, lambda b, pt: (0, 0)),
                pl.BlockSpec((None, 2 * KV_W, WINDOW), lambda b, pt: (b, 0, 0)),
                pl.BlockSpec((SUB, NSA_KV), lambda b, pt: (blk0 + b, 0)),
            ] + [page_spec(p) for p in range(n_pages)],
            out_specs=pl.BlockSpec((None, rows, KV_W), lambda b, pt: (b, 0, 0))),
        out_shape=jax.ShapeDtypeStruct((n_seq, rows, KV_W), F32),
        compiler_params=_cparams(("parallel",)),
        name="nsa_sample",
    )(pt, qbd, grow, kv_nat, ov, emat, win_t, kv, *([cache_t] * n_pages))


def _pad_rows(a, n):
    return jnp.pad(a, ((0, n - a.shape[0]),) + ((0, 0),) * (a.ndim - 1))


def _pad_cols(a, n):
    return jnp.pad(a, ((0, 0),) * (a.ndim - 1) + ((0, n - a.shape[-1]),))


FOX_SEGS = ((0, D_MODEL, None, ((BF16, ATTN_SCALE),)),
            (D_MODEL, 2 * D_MODEL, None, ((F32, 1.0), (BF16, 1.0))),
            (3 * D_MODEL, LANE, "log_sigmoid", ((F32, 1.0),)))


def fox_mixer(x, dims, w_in, b_f, cache_kv, cache_logf, slot, pt):
    bp, tp, bs, ts, n_p, n_s = dims
    nt = n_p + n_s
    n_pages = pt.shape[0] // bs
    w = _pad_cols(w_in, 3 * D_MODEL + LANE).astype(BF16)
    bias = jnp.zeros((1, 3 * D_MODEL + LANE), F32).at[0, 3 * D_MODEL:3 * D_MODEL + N_HEADS].set(b_f)
    q, kv, kvb, logf = proj(x, w, bias, FOX_SEGS)
    c = jnp.cumsum(logf[:n_p, :N_HEADS].reshape(bp, tp, N_HEADS), axis=1)
    c = jnp.transpose(c.reshape(bp, tp, N_HEADS // 2, 2), (0, 2, 1, 3))
    o_p = pair_attention("fox", q, kvb, kvb, lambda j: j, lambda j: N_HEADS // 2 + j, bp, tp, (c,))
    head = jnp.arange(D_MODEL, dtype=jnp.int32) // HEAD_DIM
    q_s = q[n_p:nt].reshape(bs, 1, ts, D_MODEL)
    own = (head[None, :] == jnp.arange(N_HEADS, dtype=jnp.int32)[:, None])[None, :, None, :]
    qbd = jnp.where(own, q_s, jnp.zeros_like(q_s)).reshape(bs, N_HEADS * ts, D_MODEL)
    cache_t = jnp.transpose(cache_kv, (0, 1, 3, 4, 5, 2)).reshape(
        cache_kv.shape[0], cache_kv.shape[1], 2 * D_MODEL, PAGE)
    exc, tot = logf_suffix(jnp.transpose(cache_logf[slot], (0, 2, 1)))
    o_s = fox_sample(pt, qbd, logf, kv, n_p, cache_t, exc, tot, slot, bs, n_pages)
    act = _pad_rows(jnp.concatenate([o_p, o_s.astype(BF16)], 0), x.shape[0])
    kv_p = kv[:n_p].reshape(bp, tp, 2, N_HEADS, HEAD_DIM)
    kv_s = kv[n_p:nt].reshape(bs, ts, 2, N_HEADS, HEAD_DIM)
    lf_p = logf[:n_p, :N_HEADS].reshape(bp, tp, N_HEADS)
    lf_s = logf[n_p:nt, :N_HEADS].reshape(bs, ts, N_HEADS)
    return act, (kv_p, kv_s, lf_p, lf_s)


def _rg_pad(a):
    shp = a.shape[:-1]
    a = a.reshape(shp + (RG_BLOCKS, RG_BS))
    return _pad_cols(a, RG_BSP).reshape(shp + (D_RNNP,))


RG_SEGS = ((0, D_RNNP, None, ((F32, 1.0),)), (D_RNNP, D_RNNP, None, ((F32, 1.0),)))


def rg_mixer(x, dims, w_in, conv_w, conv_b, ga_w, ga_b, gx_w, gx_b, lam, h0, conv_buf):
    bp, tp, bs, ts, n_p, n_s = dims
    nt = n_p + n_s
    w = jnp.concatenate([_rg_pad(w_in[:, :D_RNN]), _rg_pad(w_in[:, D_RNN:])], axis=1).astype(BF16)
    gb, u = proj(x, w, jnp.zeros((1, 2 * D_RNNP), F32), RG_SEGS)
    eye = jnp.eye(RG_BLOCKS, dtype=F32)

    def block_diag(wb):
        wp = jnp.pad(wb, ((0, 0), (0, RG_BSP - RG_BS), (0, RG_BSP - RG_BS)))
        return jnp.einsum("nij,nm->nimj", wp, eye).reshape(D_RNNP, D_RNNP)

    wg = jnp.concatenate([block_diag(ga_w), block_diag(gx_w)], axis=1).astype(BF16)
    weights = (_rg_pad(conv_w), _rg_pad(conv_b)[None], wg, _rg_pad(ga_b)[None],
               _rg_pad(gx_b)[None], _rg_pad(lam)[None])
    y_p, h_last = rg_prompt(gb, u, weights, bp, tp)
    prev = jnp.pad(_rg_pad(conv_buf), ((0, 0), (SUB - (CONV_W - 1), 0), (0, 0))).reshape(n_s, D_RNNP)
    h0r = jnp.broadcast_to(_rg_pad(h0)[:, None, :], (bs, ts, D_RNNP)).reshape(n_s, D_RNNP)
    y_s, h_s = rg_sample(gb, u, n_p, prev, h0r, weights)
    act = _pad_rows(jnp.concatenate([y_p, y_s], 0), x.shape[0])

    def unpad(a):
        return a.reshape(a.shape[:-1] + (RG_BLOCKS, RG_BSP))[..., :RG_BS].reshape(a.shape[:-1] + (D_RNN,))

    u_p = u[:n_p].reshape(bp, tp, D_RNNP)
    u_s = u[n_p:nt].reshape(bs, ts, D_RNNP)
    states = (unpad(h_last.reshape(bp, SUB, D_RNNP)[:, SUB - 1]),
              unpad(h_s.reshape(bs, ts, D_RNNP)[:, ts - 1]),
              unpad(u_p[:, tp - (CONV_W - 1):]), unpad(u_s[:, ts - (CONV_W - 1):]))
    return act, states


NSA_KV = 6 * KV_W
NSA_SEGS = ((0, D_MODEL, None, ((BF16, ATTN_SCALE),)),
            (D_MODEL, NSA_KV, None, ((F32, 1.0), (BF16, 1.0))),
            (D_MODEL + NSA_KV, LANE, "sigmoid", ((F32, 1.0),)))


def _overlap(n_sel_lanes, lane0):
    c0 = jnp.arange(LANE, dtype=jnp.int32)[:, None] * CMP_STRIDE
    lane = jnp.arange(LANE, dtype=jnp.int32)[None, :]
    s0 = (lane - lane0) * SEL_BLOCK
    hit = (c0 < s0 + SEL_BLOCK) & (c0 + CMP_BLOCK > s0) & (lane >= lane0) & (lane < lane0 + n_sel_lanes)
    return hit.astype(F32)


def _dup_groups(a):
    a = a.reshape(a.shape[0], NSA_G, 1, HEAD_DIM)
    return jnp.broadcast_to(a, (a.shape[0], NSA_G, 2, HEAD_DIM)).reshape(a.shape[0], 2 * KV_W)


def nsa_mixer(x, dims, w_in, b_gate, cmp_pe, cmp_w1, cmp_b1, cmp_w2, cache_kv, win_buf, pt):
    bp, tp, bs, ts, n_p, n_s = dims
    nt = n_p + n_s
    n_pages = pt.shape[0] // bs
    past = n_pages * PAGE
    n_w = D_MODEL + NSA_KV + LANE
    w = _pad_cols(w_in, n_w).astype(BF16)
    n_gate = 3 * N_HEADS
    bias = jnp.zeros((1, n_w), F32).at[0, D_MODEL + NSA_KV:D_MODEL + NSA_KV + n_gate].set(b_gate)
    q, kv, kvb, gates = proj(x, w, bias, NSA_SEGS)
    n_rp = CMP_STRIDE // 2
    eye2 = jnp.eye(2, dtype=F32)
    w1 = cmp_w1.reshape(2, 2, n_rp, 2, HEAD_DIM, CMP_HIDDEN)
    w1 = jnp.einsum("whpidn,gm->whpigdmn", w1, eye2)
    w1 = w1.reshape(2 * 2 * n_rp, 4 * HEAD_DIM, 2 * CMP_HIDDEN).astype(BF16)
    b1 = jnp.concatenate([cmp_b1, cmp_b1], axis=-1)[:, None, :]
    pe = jnp.broadcast_to(cmp_pe.reshape(2, 2, n_rp, 2, 1, HEAD_DIM), (2, 2, n_rp, 2, 2, HEAD_DIM))
    pe = pe.reshape(2 * 2 * n_rp, 1, 4 * HEAD_DIM)
    w2d = jnp.concatenate([cmp_w2, cmp_w2], axis=-1).astype(BF16)
    slab = jnp.eye(NSA_G, dtype=F32)
    w2s = jnp.einsum("wkd,gm->wgkmd", cmp_w2, slab).reshape(2, NSA_G, CMP_HIDDEN, KV_W).astype(BF16)
    kc_dup, _ = compress(kv, bp, tp, w1, b1, pe, w2d, w2s)
    ov_p = jnp.stack([_overlap(LANE // NSA_G, g * (LANE // NSA_G)) for g in range(NSA_G)])
    o_c, selmask = cmp_prompt(q, kc_dup[:, 0], kc_dup[:, 1], gates, ov_p, bp, tp)
    nq = tp // ATT_T
    key_blk = jnp.arange(tp, dtype=jnp.int32) // SEL_BLOCK
    lane = jnp.arange(LANE, dtype=jnp.int32)
    emat = (lane[None, :, None] == (jnp.arange(NSA_G, dtype=jnp.int32)[:, None, None] * (LANE // NSA_G)
                                    + key_blk[None, None, :])).astype(BF16)
    emat = jnp.transpose(emat.reshape(NSA_G, LANE, nq, ATT_T), (0, 2, 1, 3))
    kvb_p = kvb[:n_p]
    k_sel, v_sel, k_win, v_win = (_dup_groups(kvb_p[:, i * KV_W:(i + 1) * KV_W]) for i in range(2, 6))
    grp = lambda j: j // 2
    o_s = pair_attention("sel", q, k_sel, v_sel, grp, grp, bp, tp, (selmask, emat, gates))
    o_w = pair_attention("win", q, k_win, v_win, grp, grp, bp, tp, (gates,))
    cache_t = jnp.transpose(cache_kv, (0, 2, 3, 4, 1)).reshape(cache_kv.shape[0], 4 * KV_W, PAGE)
    _, kv_nat = compress_pages(pt, cache_t, bs, n_pages, w1, b1, pe, w2d, w2s)
    q_s = q[n_p:nt].reshape(bs, ts, N_HEADS, 1, HEAD_DIM)
    grp_of_head = jnp.arange(N_HEADS, dtype=jnp.int32) // NSA_HPG
    own = (grp_of_head[:, None] == jnp.arange(NSA_G, dtype=jnp.int32)[None, :])[None, None, :, :, None]
    qbd = jnp.where(own, q_s, jnp.zeros_like(q_s))
    qbd = jnp.transpose(qbd, (0, 2, 1, 3, 4)).reshape(bs, N_HEADS * ts, KV_W)
    g_s = gates[n_p:nt, :n_gate].reshape(bs, ts, 3, N_HEADS)
    grow = _pad_cols(jnp.transpose(g_s, (0, 3, 1, 2)).reshape(bs, N_HEADS * ts, 3), LANE)
    n_sel = -(-(past + ts) // SEL_BLOCK)
    ov_s = _overlap(n_sel, 0)
    keys = jnp.arange((n_pages + 1) * PAGE, dtype=jnp.int32) // SEL_BLOCK
    emat_s = (lane[:, None] == keys[None, :]).astype(BF16)
    win_t = jnp.transpose(win_buf, (0, 2, 3, 4, 1)).reshape(bs, 2 * KV_W, win_buf.shape[1])
    kv_s = kv[n_p:nt].reshape(bs, ts, NSA_KV)
    y_rows = nsa_sample(pt, qbd, grow, kv_nat, ov_s, emat_s, win_t, kv, n_p, cache_t, bs, n_pages)
    y5 = y_rows.reshape(bs, NSA_G, NSA_HPG, ts, NSA_G, HEAD_DIM)
    y_s = jnp.stack([y5[:, g, :, :, g, :] for g in range(NSA_G)], axis=1)
    y_s = jnp.transpose(y_s, (0, 3, 1, 2, 4)).reshape(n_s, D_MODEL).astype(BF16)
    pad = x.shape[0] - n_p
    acts = [_pad_rows(jnp.concatenate([o_c, y_s], 0), x.shape[0]),
            _pad_rows(o_s, x.shape[0]), _pad_rows(o_w, x.shape[0])]
    kv_p6 = kv[:n_p].reshape(bp, tp, 6, NSA_G, HEAD_DIM)
    kv_s6 = kv_s.reshape(bs, ts, 6, NSA_G, HEAD_DIM)
    w_keep = min(WINDOW, tp)
    win_new = jnp.concatenate([win_buf, kv_s6[:, :, 4:]], axis=1)[:, ts:]
    return acts, (kv_p6[:, :, :4], kv_s6[:, :, :4], kv_p6[:, tp - w_keep:, 4:], win_new)


def kernel(x_prompt, x_sample, cache_fox_kv, cache_fox_logf, state_rg_h, state_rg_conv, cache_nsa_kv, state_nsa_win, page_table, fox_w_in, fox_b_f, fox_w_out, rg_w_in, rg_conv_w, rg_conv_b, rg_gate_a_w, rg_gate_a_b, rg_gate_x_w, rg_gate_x_b, rg_lambda, rg_w_out, nsa_w_in, nsa_b_gate, nsa_cmp_pe, nsa_cmp_w1, nsa_cmp_b1, nsa_cmp_w2, nsa_w_out, ffn_w_gu, ffn_w_down, moe_router_w, moe_router_b, moe_w_gu, moe_w_down, ln_mix_g, ln_mix_b, ln_ffn_g, ln_ffn_b):
    bp, tp, d = x_prompt.shape
    bs, ts, _ = x_sample.shape
    assert ts == SUB and tp % ATT_T == 0 and WINDOW == ATT_T
    n_p, n_s = bp * tp, bs * ts
    nt = n_p + n_s
    dims = (bp, tp, bs, ts, n_p, n_s)
    ntp = -(-nt // TOKEN_TILE) * TOKEN_TILE
    x = _pad_rows(jnp.concatenate([x_prompt.reshape(n_p, d), x_sample.reshape(n_s, d)], 0), ntp)
    pt = page_table.reshape(-1)
    moe_gu, moe_down = moe_w_gu.astype(BF16), moe_w_down.astype(BF16)
    fox_out, rg_out, nsa_out = [], [], []
    for i in range(DEPTH):
        kind, slot = i % 3, i // 3
        if kind == 0:
            act, new = fox_mixer(x, dims, fox_w_in[slot], fox_b_f[slot], cache_fox_kv,
                                 cache_fox_logf, slot, pt)
            fox_out.append(new)
            acts, w_out = [act], fox_w_out[slot]
        elif kind == 1:
            act, new = rg_mixer(x, dims, rg_w_in[slot], rg_conv_w[slot], rg_conv_b[slot],
                                rg_gate_a_w[slot], rg_gate_a_b[slot], rg_gate_x_w[slot],
                                rg_gate_x_b[slot], rg_lambda[slot], state_rg_h[slot],
                                state_rg_conv[slot])
            rg_out.append(new)
            acts = [act]
            w_out = jnp.pad(rg_w_out[slot].reshape(RG_BLOCKS, RG_BS, d),
                            ((0, 0), (0, RG_BSP - RG_BS), (0, 0))).reshape(D_RNNP, d)
        else:
            acts, new = nsa_mixer(x, dims, nsa_w_in[slot], nsa_b_gate[slot], nsa_cmp_pe[slot],
                                  nsa_cmp_w1[slot], nsa_cmp_b1[slot], nsa_cmp_w2[slot],
                                  cache_nsa_kv[slot], state_nsa_win[slot], pt)
            nsa_out.append(new)
            w_out = nsa_w_out[slot]
        w_o = w_out.astype(BF16)
        j = i // 2
        g_m, b_m = ln_mix_g[i][None], ln_mix_b[i][None]
        g_f, b_f = ln_ffn_g[i][None], ln_ffn_b[i][None]
        if i % 2 == 0:
            x = out_ln(acts, w_o, x, g_m, b_m)
            x = ffn_ln(x, ffn_w_gu[j].astype(BF16), ffn_w_down[j].astype(BF16), g_f, b_f)
        else:
            rw_p = _pad_cols(moe_router_w[j], LANE)
            rb_p = _pad_cols(moe_router_b[j][None], LANE)
            x, idx, wt = out_ln(acts, w_o, x, g_m, b_m, router=(rw_p, rb_p))
            x = moe_ln(x, idx, wt, moe_gu, moe_down, j, g_f, b_f)
    xp = x[:n_p].reshape(bp, tp, d)
    xs = x[n_p:nt].reshape(bs, ts, d)
    stack = lambda outs, k: jnp.stack([o[k] for o in outs])
    return (xp, xs,
            stack(fox_out, 0), stack(fox_out, 1), stack(fox_out, 2), stack(fox_out, 3),
            stack(rg_out, 0), stack(rg_out, 1), stack(rg_out, 2), stack(rg_out, 3),
            stack(nsa_out, 0), stack(nsa_out, 1), stack(nsa_out, 2), stack(nsa_out, 3))
```
